```python
import math
import jax, jax.numpy as jnp
from jax import lax
import numpy as np

D_MODEL = 2048
BATCH = 8
SEQ = 2048
DEPTH = 2
DEC_BATCH = 128
DEC_SEQ = 8
PAST_LEN = 2048
PAGE_SIZE = 128

HEAD_DIM = 128
D_MIX = D_MODEL
G_A = D_MIX // 4
G_B = D_MIX // 4
G_C = D_MIX // 4
G_D = D_MIX // 4
SGU_HEADS = G_B // HEAD_DIM
ATT_HEADS = G_D // HEAD_DIM
ATT_V_DIM = HEAD_DIM
ATT_QK_DIM = HEAD_DIM // 2
ATT_SCALE = ATT_QK_DIM ** -0.5
SC_WIDTH = 3
CONF_WIDTH = 31
CHUNK = 128
Q_BLOCK = 128
D_FF = -(-8 * D_MODEL // (3 * 256)) * 256
DN_ALPHA = (2 * DEPTH) ** 0.25
DN_BETA = (8 * DEPTH) ** -0.25
LN_EPS = 1e-5
NEG_INF = -1e30

SPLIT_WIDTHS = (G_A, G_A, G_A, G_B, G_B, G_C, G_C,
                ATT_HEADS * 2 * ATT_QK_DIM, ATT_HEADS * 2 * ATT_QK_DIM, ATT_HEADS * ATT_V_DIM)
D_IN = sum(SPLIT_WIDTHS)
SPLIT_POINTS = tuple(sum(SPLIT_WIDTHS[:i + 1]) for i in range(len(SPLIT_WIDTHS) - 1))

kernel_name = "hymba_style_hybrid_decode_step"


def layer_norm(x, g, b):
    xf = x.astype(jnp.float32)
    mu = jnp.mean(xf, axis=-1, keepdims=True)
    var = jnp.mean(jnp.square(xf - mu), axis=-1, keepdims=True)
    y = (xf - mu) * lax.rsqrt(var + LN_EPS) * g.astype(jnp.float32) + b.astype(jnp.float32)
    return y.astype(x.dtype)


def rms_norm(x, g):
    xf = x.astype(jnp.float32)
    y = xf * lax.rsqrt(jnp.mean(jnp.square(xf), axis=-1, keepdims=True) + LN_EPS) * g.astype(jnp.float32)
    return y.astype(x.dtype)


def depthwise_causal_conv(x_ext, w):
    c = x_ext.shape[-1]
    return lax.conv_general_dilated(x_ext, w[:, None, :], window_strides=(1,), padding='VALID',
                                    dimension_numbers=('NWC', 'WIO', 'NWC'), feature_group_count=c)


def chunk_spatial_gate(v_n, w_s, b_s):
    bsz, t, _ = v_n.shape
    n = -(-t // CHUNK)
    v_pad = jnp.pad(v_n, ((0, 0), (0, n * CHUNK - t), (0, 0))).reshape(bsz, n, CHUNK, SGU_HEADS, HEAD_DIM)
    tril = jnp.tril(jnp.ones((CHUNK, CHUNK), dtype=bool))
    w_c = jnp.where(tril[None], w_s, jnp.zeros_like(w_s))
    mixed = jnp.einsum('hts,bnshc->bnthc', w_c, v_pad) + jnp.transpose(b_s)[None, None, :, :, None]
    return mixed.reshape(bsz, n * CHUNK, G_B)[:, :t]


def diff_attention(q, k, v, q_offset):
    bsz, t = q.shape[:2]
    blk = min(t, Q_BLOCK)
    nb = t // blk
    qb = jnp.moveaxis(q.reshape(bsz, nb, blk, ATT_HEADS, 2, ATT_QK_DIM), 1, 0)
    k_pos = jnp.arange(k.shape[1])

    def block(args):
        q_blk, i = args
        s = jnp.einsum('bqhmd,bkhmd->bhmqk', q_blk, k).astype(jnp.float32) * ATT_SCALE
        q_pos = q_offset + i * blk + jnp.arange(blk)
        s = jnp.where(k_pos[None, :] <= q_pos[:, None], s, NEG_INF)
        p = jax.nn.softmax(s, axis=-1).astype(v.dtype)
        return jnp.einsum('bhmqk,bkhd->bqhmd', p, v)

    o = lax.map(block, (qb, jnp.arange(nb)))
    return jnp.moveaxis(o, 0, 1).reshape(bsz, t, ATT_HEADS, 2, ATT_V_DIM)


def mixing_sublayer(x, hist_a, hist_c, k_past, v_past, p, lam_init):
    bsz, t, _ = x.shape
    proj = jnp.einsum('btd,de->bte', x, p['w_in'])
    a_b, a_c, a_h, s_u, s_v, c_a, c_g, q, k, v = jnp.split(proj, SPLIT_POINTS, axis=-1)
    ch = a_c * a_h
    ch_ext = jnp.concatenate([hist_a, ch], axis=1)
    y_a = a_b * depthwise_causal_conv(ch_ext, p['conv_a_w'])
    v_n = layer_norm(s_v, p['sgu_ln_g'], p['sgu_ln_b'])
    y_b = s_u * chunk_spatial_gate(v_n, p['sgu_w'], p['sgu_b'])
    z = c_a * jax.nn.sigmoid(c_g)
    z_ext = jnp.concatenate([hist_c, z], axis=1)
    zc = depthwise_causal_conv(z_ext, p['conf_w']) + p['conf_b']
    y_c = jax.nn.silu(layer_norm(zc, p['conf_ln_g'], p['conf_ln_b']))
    q = q.reshape(bsz, t, ATT_HEADS, 2, ATT_QK_DIM)
    k = k.reshape(bsz, t, ATT_HEADS, 2, ATT_QK_DIM)
    v = v.reshape(bsz, t, ATT_HEADS, ATT_V_DIM)
    k_all = jnp.concatenate([k_past, k], axis=1)
    v_all = jnp.concatenate([v_past, v], axis=1)
    o = diff_attention(q, k_all, v_all, k_past.shape[1])
    lam = (jnp.exp(jnp.sum(p['lam_q1'].astype(jnp.float32) * p['lam_k1'].astype(jnp.float32)))
           - jnp.exp(jnp.sum(p['lam_q2'].astype(jnp.float32) * p['lam_k2'].astype(jnp.float32))) + lam_init)
    od = o[..., 0, :].astype(jnp.float32) - lam * o[..., 1, :].astype(jnp.float32)
    od = rms_norm(od, p['subln_g']) * (1.0 - lam_init)
    y_d = od.reshape(bsz, t, G_D).astype(x.dtype)
    y = jnp.concatenate([y_a, y_b, y_c, y_d], axis=-1)
    out = jnp.einsum('bte,ed->btd', y, p['w_out'])
    return out, ch_ext[:, -(SC_WIDTH - 1):], z_ext[:, -(CONF_WIDTH - 1):], k, v, v_n


def swiglu(x, w_gate, w_up, w_down):
    h = jax.nn.silu(jnp.einsum('btd,df->btf', x, w_gate)) * jnp.einsum('btd,df->btf', x, w_up)
    return jnp.einsum('btf,fd->btd', h, w_down)


def trunk_layer(x, hist_a, hist_c, k_past, v_past, p, lam_init):
    mix, st_a, st_c, k_new, v_new, v_n = mixing_sublayer(x, hist_a, hist_c, k_past, v_past, p, lam_init)
    x = layer_norm(DN_ALPHA * x + mix, p['ln1_g'], p['ln1_b'])
    x = layer_norm(DN_ALPHA * x + swiglu(x, p['w_gate'], p['w_up'], p['w_down']), p['ln2_g'], p['ln2_b'])
    return x, st_a, st_c, k_new, v_new, v_n


def setup_inputs(seed: int = 0) -> dict:
    key = jax.random.key(seed)
    ks = jax.random.split(key, 32)
    n_pages = PAST_LEN // PAGE_SIZE
    n_pool = (DEC_BATCH * n_pages * 5) // 4

    def nrm(k, shape, s):
        return jax.random.normal(k, shape, jnp.float32) * s

    def gain(k, shape):
        return 1.0 + nrm(k, shape, 0.02)

    perm = jax.random.permutation(ks[7], n_pool)
    page_table = perm[:DEC_BATCH * n_pages].reshape(DEC_BATCH, n_pages).astype(jnp.int32)
    return {
        'x_prompt': nrm(ks[0], (BATCH, SEQ, D_MODEL), 1.0),
        'x_sample': nrm(ks[1], (DEC_BATCH, DEC_SEQ, D_MODEL), 1.0),
        'cache_k': nrm(ks[2], (DEPTH, n_pool, PAGE_SIZE, ATT_HEADS, 2, ATT_QK_DIM), 1.0),
        'cache_v': nrm(ks[3], (DEPTH, n_pool, PAGE_SIZE, ATT_HEADS, ATT_V_DIM), 1.0),
        'state_conv_a': nrm(ks[4], (DEPTH, DEC_BATCH, SC_WIDTH - 1, G_A), 1.0),
        'state_conv_c': nrm(ks[5], (DEPTH, DEC_BATCH, CONF_WIDTH - 1, G_C), 1.0),
        'page_table': page_table,
        'w_in': nrm(ks[8], (DEPTH, D_MODEL, D_IN), D_MODEL ** -0.5),
        'w_out': nrm(ks[9], (DEPTH, D_MIX, D_MODEL), DN_BETA * D_MIX ** -0.5),
        'conv_a_w': nrm(ks[10], (DEPTH, SC_WIDTH, G_A), SC_WIDTH ** -0.5),
        'sgu_ln_g': gain(ks[11], (DEPTH, G_B)),
        'sgu_ln_b': nrm(ks[12], (DEPTH, G_B), 0.02),
        'sgu_w': nrm(ks[13], (DEPTH, SGU_HEADS, CHUNK, CHUNK), CHUNK ** -0.5),
        'sgu_b': gain(ks[14], (DEPTH, SGU_HEADS, CHUNK)),
        'conf_w': nrm(ks[15], (DEPTH, CONF_WIDTH, G_C), CONF_WIDTH ** -0.5),
        'conf_b': nrm(ks[16], (DEPTH, G_C), 0.02),
        'conf_ln_g': gain(ks[17], (DEPTH, G_C)),
        'conf_ln_b': nrm(ks[18], (DEPTH, G_C), 0.02),
        'lam_q1': nrm(ks[19], (DEPTH, ATT_QK_DIM), 0.1),
        'lam_k1': nrm(ks[20], (DEPTH, ATT_QK_DIM), 0.1),
        'lam_q2': nrm(ks[21], (DEPTH, ATT_QK_DIM), 0.1),
        'lam_k2': nrm(ks[22], (DEPTH, ATT_QK_DIM), 0.1),
        'subln_g': gain(ks[23], (DEPTH, ATT_V_DIM)),
        'ln1_g': gain(ks[24], (DEPTH, D_MODEL)),
        'ln1_b': nrm(ks[25], (DEPTH, D_MODEL), 0.02),
        'w_gate': nrm(ks[26], (DEPTH, D_MODEL, D_FF), D_MODEL ** -0.5),
        'w_up': nrm(ks[27], (DEPTH, D_MODEL, D_FF), D_MODEL ** -0.5),
        'w_down': nrm(ks[28], (DEPTH, D_FF, D_MODEL), DN_BETA * D_FF ** -0.5),
        'ln2_g': gain(ks[29], (DEPTH, D_MODEL)),
        'ln2_b': nrm(ks[30], (DEPTH, D_MODEL), 0.02),
    }


def reference(x_prompt, x_sample, cache_k, cache_v, state_conv_a, state_conv_c, page_table,
              w_in, w_out, conv_a_w, sgu_ln_g, sgu_ln_b, sgu_w, sgu_b, conf_w, conf_b, conf_ln_g, conf_ln_b,
              lam_q1, lam_k1, lam_q2, lam_k2, subln_g, ln1_g, ln1_b, w_gate, w_up, w_down, ln2_g, ln2_b):
    bp, dt = x_prompt.shape[0], x_prompt.dtype
    bd = x_sample.shape[0]
    past = page_table.shape[1] * cache_k.shape[2]
    xp, xs = x_prompt, x_sample
    kp_l, vp_l, ks_l, vs_l, ap_l, as_l, cp_l, cs_l, us_l = [], [], [], [], [], [], [], [], []
    for l in range(DEPTH):
        p = {'w_in': w_in[l], 'w_out': w_out[l], 'conv_a_w': conv_a_w[l], 'sgu_ln_g': sgu_ln_g[l],
             'sgu_ln_b': sgu_ln_b[l], 'sgu_w': sgu_w[l], 'sgu_b': sgu_b[l], 'conf_w': conf_w[l],
             'conf_b': conf_b[l], 'conf_ln_g': conf_ln_g[l], 'conf_ln_b': conf_ln_b[l],
             'lam_q1': lam_q1[l], 'lam_k1': lam_k1[l], 'lam_q2': lam_q2[l], 'lam_k2': lam_k2[l],
             'subln_g': subln_g[l], 'ln1_g': ln1_g[l], 'ln1_b': ln1_b[l], 'w_gate': w_gate[l],
             'w_up': w_up[l], 'w_down': w_down[l], 'ln2_g': ln2_g[l], 'ln2_b': ln2_b[l]}
        lam_init = 0.8 - 0.6 * math.exp(-0.3 * l)
        xp, a_p, c_p, k_p, v_p, _ = trunk_layer(
            xp,
            jnp.zeros((bp, SC_WIDTH - 1, G_A), dt),
            jnp.zeros((bp, CONF_WIDTH - 1, G_C), dt),
            jnp.zeros((bp, 0, ATT_HEADS, 2, ATT_QK_DIM), dt),
            jnp.zeros((bp, 0, ATT_HEADS, ATT_V_DIM), dt),
            p, lam_init)
        k_past = cache_k[l][page_table].reshape(bd, past, ATT_HEADS, 2, ATT_QK_DIM)
        v_past = cache_v[l][page_table].reshape(bd, past, ATT_HEADS, ATT_V_DIM)
        xs, a_s, c_s, k_s, v_s, u_s = trunk_layer(xs, state_conv_a[l], state_conv_c[l], k_past, v_past, p, lam_init)
        kp_l.append(k_p); vp_l.append(v_p); ks_l.append(k_s); vs_l.append(v_s)
        ap_l.append(a_p); as_l.append(a_s); cp_l.append(c_p); cs_l.append(c_s); us_l.append(u_s)
    return (xp, xs, jnp.stack(kp_l), jnp.stack(vp_l), jnp.stack(ks_l), jnp.stack(vs_l),
            jnp.stack(ap_l), jnp.stack(as_l), jnp.stack(cp_l), jnp.stack(cs_l), jnp.stack(us_l))
```

```python
import functools
import math

import jax
import jax.numpy as jnp
from jax import lax
from jax.experimental import pallas as pl
from jax.experimental.pallas import tpu as pltpu

D_MODEL = 2048
DEPTH = 2
HEAD_DIM = 128
G = D_MODEL // 4
HEADS = G // HEAD_DIM
QK_DIM = HEAD_DIM // 2
ATT_SCALE = QK_DIM ** -0.5
SC_WIDTH = 3
CONF_WIDTH = 31
CHUNK = 128
D_FF = -(-8 * D_MODEL // (3 * 256)) * 256
DN_ALPHA = (2 * DEPTH) ** 0.25
LN_EPS = 1e-5
NEG_INF = -1e30
N_SPLIT = 10
(COL_AB, COL_AC, COL_AH, COL_SU, COL_SV, COL_CA, COL_CG, COL_Q, COL_K, COL_V) = range(N_SPLIT)

F32 = jnp.float32
BF16 = jnp.bfloat16
NT_DIMS = (((1,), (1,)), ((), ()))

VMEM_LIMIT = 56 * 1024 * 1024
HALO = 32
CONV_ROWS = 64


def _params(*sem):
    return pltpu.CompilerParams(dimension_semantics=sem, vmem_limit_bytes=VMEM_LIMIT)


def _layer_norm(x, g, b):
    mu = jnp.mean(x, axis=-1, keepdims=True)
    xc = x - mu
    var = jnp.mean(xc * xc, axis=-1, keepdims=True)
    return xc * lax.rsqrt(var + LN_EPS) * g + b


def _sigmoid(x):
    return 1.0 / (1.0 + jnp.exp(-x))


def _lambda(lam_ref, lam_init):
    lp = lam_ref[...]
    a = jnp.sum(lp[0:1] * lp[1:2], axis=-1, keepdims=True)
    b = jnp.sum(lp[2:3] * lp[3:4], axis=-1, keepdims=True)
    return jnp.exp(a) - jnp.exp(b) + lam_init


def _diff_norm(o0, o1, lam, g, lam_init):
    od = o0 - lam * o1
    ms = jnp.mean(od * od, axis=-1, keepdims=True)
    return od * lax.rsqrt(ms + LN_EPS) * g * (1.0 - lam_init)


def _inproj_kernel(x_ref, w_ref, o_ref, xb_ref):
    @pl.when(pl.program_id(1) == 0)
    def _():
        xb_ref[...] = x_ref[...].astype(BF16)

    o_ref[...] = jnp.dot(xb_ref[...], w_ref[...], preferred_element_type=F32)


def _inproj(x, w, tm):
    n = x.shape[0]
    return pl.pallas_call(
        _inproj_kernel,
        grid=(n // tm, N_SPLIT),
        in_specs=[pl.BlockSpec((tm, D_MODEL), lambda i, j: (i, 0)),
                  pl.BlockSpec((D_MODEL, G), lambda i, j: (0, j))],
        out_specs=pl.BlockSpec((None, tm, G), lambda i, j: (j, i, 0)),
        out_shape=jax.ShapeDtypeStruct((N_SPLIT, n, G), F32),
        scratch_shapes=[pltpu.VMEM((tm, D_MODEL), BF16)],
        compiler_params=_params("parallel", "arbitrary"),
        name="inproj",
    )(x, w)


def _mix_prompt_kernel(ab_ref, ac_ref, ah_ref, su_ref, sv_ref, ca_ref, cg_ref,
                       hac_ref, hah_ref, hca_ref, hcg_ref,
                       caw_ref, slg_ref, slb_ref, sw_ref, sbt_ref,
                       cw_ref, cb_ref, clg_ref, clb_ref,
                       y_ref, sta_ref, stc_ref,
                       chx_ref, zx_ref, vn_ref, *, tt):
    first = pl.program_id(1) == 0

    ch = ac_ref[...] * ah_ref[...]
    chx_ref[8:8 + tt, :] = ch
    prev = hac_ref[HALO - 8:HALO, :] * hah_ref[HALO - 8:HALO, :]
    chx_ref[0:8, :] = jnp.where(first, 0.0, prev)
    caw = caw_ref[...]
    conv = (caw[0:1] * chx_ref[6:6 + tt, :] + caw[1:2] * chx_ref[7:7 + tt, :]
            + caw[2:3] * chx_ref[8:8 + tt, :])
    y_ref[:, 0:G] = ab_ref[...] * conv
    sta_ref[...] = chx_ref[8 + tt - (SC_WIDTH - 1):8 + tt, :]

    vn_ref[...] = _layer_norm(sv_ref[...], slg_ref[...], slb_ref[...])
    r_i = lax.broadcasted_iota(jnp.int32, (CHUNK, CHUNK), 0)
    c_i = lax.broadcasted_iota(jnp.int32, (CHUNK, CHUNK), 1)
    for h in range(HEADS):
        lanes = slice(h * HEAD_DIM, (h + 1) * HEAD_DIM)
        w_c = jnp.where(c_i <= r_i, sw_ref[h], 0.0).astype(BF16)
        bias = sbt_ref[:, h:h + 1]
        for n in range(tt // CHUNK):
            rows = slice(n * CHUNK, (n + 1) * CHUNK)
            mixed = jnp.dot(w_c, vn_ref[rows, lanes].astype(BF16),
                            preferred_element_type=F32) + bias
            y_ref[rows, G + h * HEAD_DIM:G + (h + 1) * HEAD_DIM] = su_ref[rows, lanes] * mixed

    zx_ref[HALO:HALO + tt, :] = ca_ref[...] * _sigmoid(cg_ref[...])
    zprev = hca_ref[...] * _sigmoid(hcg_ref[...])
    zx_ref[0:HALO, :] = jnp.where(first, 0.0, zprev)
    stc_ref[...] = zx_ref[HALO + tt - (CONF_WIDTH - 1):HALO + tt, :]
    base = HALO - (CONF_WIDTH - 1)
    for c in range(tt // CONV_ROWS):
        r0 = c * CONV_ROWS
        acc = None
        for k in range(CONF_WIDTH):
            term = cw_ref[k:k + 1, :] * zx_ref[r0 + base + k:r0 + base + k + CONV_ROWS, :]
            acc = term if acc is None else acc + term
        zn = _layer_norm(acc + cb_ref[...], clg_ref[...], clb_ref[...])
        y_ref[r0:r0 + CONV_ROWS, 2 * G:3 * G] = zn * _sigmoid(zn)


def _mix_prompt(proj, p, batch, seq, tt=256):
    proj4 = proj.reshape(N_SPLIT, batch, seq, G)
    hb = tt // HALO

    def col(c):
        return pl.BlockSpec((None, None, tt, G), lambda b, t, c=c: (c, b, t, 0))

    def halo(c):
        return pl.BlockSpec((None, None, HALO, G),
                            lambda b, t, c=c: (c, b, jnp.maximum(t * hb - 1, 0), 0))

    def full(a):
        return pl.BlockSpec(a.shape, lambda b, t, nd=a.ndim: (0,) * nd)

    weights = [p['conv_a_w'], p['sgu_ln_g'], p['sgu_ln_b'], p['sgu_w'], p['sgu_b_t'],
               p['conf_w'], p['conf_b'], p['conf_ln_g'], p['conf_ln_b']]
    return pl.pallas_call(
        functools.partial(_mix_prompt_kernel, tt=tt),
        grid=(batch, seq // tt),
        in_specs=[col(c) for c in (COL_AB, COL_AC, COL_AH, COL_SU, COL_SV, COL_CA, COL_CG)]
                 + [halo(c) for c in (COL_AC, COL_AH, COL_CA, COL_CG)]
                 + [full(a) for a in weights],
        out_specs=[pl.BlockSpec((None, tt, 3 * G), lambda b, t: (b, t, 0)),
                   pl.BlockSpec((None, SC_WIDTH - 1, G), lambda b, t: (b, 0, 0)),
                   pl.BlockSpec((None, CONF_WIDTH - 1, G), lambda b, t: (b, 0, 0))],
        out_shape=[jax.ShapeDtypeStruct((batch, seq, 3 * G), F32),
                   jax.ShapeDtypeStruct((batch, SC_WIDTH - 1, G), F32),
                   jax.ShapeDtypeStruct((batch, CONF_WIDTH - 1, G), F32)],
        scratch_shapes=[pltpu.VMEM((8 + tt, G), F32), pltpu.VMEM((HALO + tt, G), F32),
                        pltpu.VMEM((tt, G), F32)],
        compiler_params=_params("parallel", "arbitrary"),
        name="mix_prompt",
    )(*([proj4] * 11), *weights)


def _mix_sample_kernel(ab_ref, ac_ref, ah_ref, su_ref, sv_ref, ca_ref, cg_ref,
                       ha_ref, hc_ref,
                       caw_ref, slg_ref, slb_ref, swd_ref, sbc_ref,
                       cw_ref, cb_ref, clg_ref, clb_ref,
                       y_ref, sta_ref, stc_ref, vn_ref,
                       ch_ref, z_ref, zc_ref, chx_ref, zx_ref, *, bs, t_new):
    ka, kc = SC_WIDTH - 1, CONF_WIDTH - 1

    ch_ref[...] = ac_ref[...] * ah_ref[...]
    z_ref[...] = ca_ref[...] * _sigmoid(cg_ref[...])
    vn_ref[...] = _layer_norm(sv_ref[...], slg_ref[...], slb_ref[...])

    for h in range(HEADS):
        lanes = slice(h * HEAD_DIM, (h + 1) * HEAD_DIM)
        mixed = jnp.dot(swd_ref[h].astype(BF16), vn_ref[:, lanes].astype(BF16),
                        preferred_element_type=F32) + sbc_ref[:, h:h + 1]
        y_ref[:, G + h * HEAD_DIM:G + (h + 1) * HEAD_DIM] = su_ref[:, lanes] * mixed

    caw = caw_ref[...]

    def per_sample(b, carry):
        r = pl.multiple_of(b * t_new, t_new)
        new = pl.ds(r, t_new)
        chx_ref[8 - ka:8, :] = ha_ref[b]
        chx_ref[8:8 + t_new, :] = ch_ref[new, :]
        conv = (caw[0:1] * chx_ref[6:6 + t_new, :] + caw[1:2] * chx_ref[7:7 + t_new, :]
                + caw[2:3] * chx_ref[8:8 + t_new, :])
        y_ref[new, 0:G] = ab_ref[new, :] * conv
        sta_ref[b] = chx_ref[8 + t_new - ka:8 + t_new, :]

        zx_ref[HALO - kc:HALO, :] = hc_ref[b]
        zx_ref[HALO:HALO + t_new, :] = z_ref[new, :]
        acc = None
        for k in range(CONF_WIDTH):
            term = cw_ref[k:k + 1, :] * zx_ref[HALO - kc + k:HALO - kc + k + t_new, :]
            acc = term if acc is None else acc + term
        zc_ref[new, :] = acc
        stc_ref[b] = zx_ref[HALO + t_new - kc:HALO + t_new, :]
        return carry

    lax.fori_loop(0, bs, per_sample, 0)
    zn = _layer_norm(zc_ref[...] + cb_ref[...], clg_ref[...], clb_ref[...])
    y_ref[:, 2 * G:3 * G] = zn * _sigmoid(zn)


def _mix_sample(proj, hist_a, hist_c, p, batch, t_new, bs=16):
    ka, kc = SC_WIDTH - 1, CONF_WIDTH - 1
    n = batch * t_new
    rows = bs * t_new
    tril = jnp.tril(jnp.ones((t_new, t_new), F32))
    w8 = p['sgu_w'][:, :t_new, :t_new] * tril
    swd = jax.vmap(lambda w: jnp.kron(jnp.eye(bs, dtype=F32), w))(w8)
    sbc = jnp.tile(p['sgu_b'][:, :t_new], (1, bs)).T

    def col(c):
        return pl.BlockSpec((None, rows, G), lambda i, c=c: (c, i, 0))

    def full(a):
        return pl.BlockSpec(a.shape, lambda i, nd=a.ndim: (0,) * nd)

    weights = [p['conv_a_w'], p['sgu_ln_g'], p['sgu_ln_b'], swd, sbc,
               p['conf_w'], p['conf_b'], p['conf_ln_g'], p['conf_ln_b']]
    return pl.pallas_call(
        functools.partial(_mix_sample_kernel, bs=bs, t_new=t_new),
        grid=(batch // bs,),
        in_specs=[col(c) for c in (COL_AB, COL_AC, COL_AH, COL_SU, COL_SV, COL_CA, COL_CG)]
                 + [pl.BlockSpec((bs, ka, G), lambda i: (i, 0, 0)),
                    pl.BlockSpec((bs, kc, G), lambda i: (i, 0, 0))]
                 + [full(a) for a in weights],
        out_specs=[pl.BlockSpec((rows, 3 * G), lambda i: (i, 0)),
                   pl.BlockSpec((bs, ka, G), lambda i: (i, 0, 0)),
                   pl.BlockSpec((bs, kc, G), lambda i: (i, 0, 0)),
                   pl.BlockSpec((rows, G), lambda i: (i, 0))],
        out_shape=[jax.ShapeDtypeStruct((n, 3 * G), F32),
                   jax.ShapeDtypeStruct((batch, ka, G), F32),
                   jax.ShapeDtypeStruct((batch, kc, G), F32),
                   jax.ShapeDtypeStruct((n, G), F32)],
        scratch_shapes=[pltpu.VMEM((rows, G), F32)] * 3
                       + [pltpu.VMEM((8 + t_new, G), F32), pltpu.VMEM((HALO + t_new, G), F32)],
        compiler_params=_params("parallel"),
        name="mix_sample",
    )(*([proj] * 7), hist_a, hist_c, *weights)


def _attn_prompt_kernel(q_ref, k_ref, v_ref, lam_ref, g_ref, o_ref, kb_ref, vb_ref,
                        *, lam_init, tq, tk):
    qi = pl.program_id(2)

    @pl.when(qi == 0)
    def _():
        kb_ref[...] = k_ref[...].astype(BF16)
        vb_ref[...] = v_ref[...].astype(BF16)

    q = q_ref[...] * ATT_SCALE
    lane = lax.broadcasted_iota(jnp.int32, (tq, HEAD_DIM), 1)
    q2 = jnp.concatenate([jnp.where(lane < QK_DIM, q, 0.0),
                          jnp.where(lane >= QK_DIM, q, 0.0)], axis=0).astype(BF16)

    def step(j, carry, masked):
        m, l, acc = carry
        off = pl.multiple_of(j * tk, tk)
        s = lax.dot_general(q2, kb_ref[pl.ds(off, tk), :], NT_DIMS, preferred_element_type=F32)
        if masked:
            q_pos = qi * tq + lax.broadcasted_iota(jnp.int32, (2 * tq, tk), 0) % tq
            k_pos = off + lax.broadcasted_iota(jnp.int32, (2 * tq, tk), 1)
            s = jnp.where(k_pos <= q_pos, s, NEG_INF)
        m_new = jnp.maximum(m, jnp.max(s, axis=-1, keepdims=True))
        alpha = jnp.exp(m - m_new)
        p = jnp.exp(s - m_new)
        l = alpha * l + jnp.sum(p, axis=-1, keepdims=True)
        acc = alpha * acc + jnp.dot(p.astype(BF16), vb_ref[pl.ds(off, tk), :],
                                    preferred_element_type=F32)
        return m_new, l, acc

    n_full = (qi * tq) // tk
    init = (jnp.full((2 * tq, 1), NEG_INF, F32), jnp.zeros((2 * tq, 1), F32),
            jnp.zeros((2 * tq, HEAD_DIM), F32))
    carry = lax.fori_loop(0, n_full, lambda j, c: step(j, c, False), init)
    _, l, acc = step(n_full, carry, True)
    o = acc / l
    o_ref[...] = _diff_norm(o[:tq], o[tq:], _lambda(lam_ref, lam_init), g_ref[...], lam_init)


def _attn_prompt(proj, p, batch, seq, lam_init, tq=256, tk=512):
    proj4 = proj.reshape(N_SPLIT, batch, seq, G)

    def kv(c):
        return pl.BlockSpec((None, None, seq, HEAD_DIM), lambda b, h, i, c=c: (c, b, 0, h))

    return pl.pallas_call(
        functools.partial(_attn_prompt_kernel, lam_init=lam_init, tq=tq, tk=tk),
        grid=(batch, HEADS, seq // tq),
        in_specs=[pl.BlockSpec((None, None, tq, HEAD_DIM), lambda b, h, i: (COL_Q, b, i, h)),
                  kv(COL_K), kv(COL_V),
                  pl.BlockSpec((4, QK_DIM), lambda b, h, i: (0, 0)),
                  pl.BlockSpec((1, HEAD_DIM), lambda b, h, i: (0, 0))],
        out_specs=pl.BlockSpec((None, tq, HEAD_DIM), lambda b, h, i: (b, i, h)),
        out_shape=jax.ShapeDtypeStruct((batch, seq, G), F32),
        scratch_shapes=[pltpu.VMEM((seq, HEAD_DIM), BF16)] * 2,
        compiler_params=_params("parallel", "parallel", "arbitrary"),
        name="attn_prompt",
    )(proj4, proj4, proj4, p['lam'], p['subln_g'])


def _attn_sample_kernel(pt_ref, q_ref, kn_ref, vn_ref, lam_ref, g_ref, *rest,
                        lam_init, t_new, n_pages, page):
    del pt_ref
    k_refs, v_refs, o_ref = rest[:n_pages], rest[n_pages:2 * n_pages], rest[2 * n_pages]
    n_rows = HEADS * 2 * t_new

    q = q_ref[...] * ATT_SCALE
    qt = jnp.concatenate([q] * (HEADS * 2), axis=0)
    r_i = lax.broadcasted_iota(jnp.int32, (n_rows, G), 0)
    c_i = lax.broadcasted_iota(jnp.int32, (n_rows, G), 1)
    qbig = jnp.where(c_i // QK_DIM == r_i // t_new, qt, 0.0).astype(BF16)

    s_past = [jnp.dot(qbig, k_refs[j][...].astype(BF16), preferred_element_type=F32)
              for j in range(n_pages)]
    pad = jnp.zeros((page - t_new, G), F32)
    s_new = lax.dot_general(qbig, jnp.concatenate([kn_ref[...], pad], axis=0).astype(BF16),
                            NT_DIMS, preferred_element_type=F32)
    r_n = lax.broadcasted_iota(jnp.int32, (n_rows, page), 0) % t_new
    c_n = lax.broadcasted_iota(jnp.int32, (n_rows, page), 1)
    s_new = jnp.where(c_n <= r_n, s_new, NEG_INF)

    m = s_new
    for s in s_past:
        m = jnp.maximum(m, s)
    m = jnp.max(m, axis=-1, keepdims=True)
    p_new = jnp.exp(s_new - m)
    l = jnp.sum(p_new, axis=-1, keepdims=True)
    acc = jnp.dot(p_new.astype(BF16),
                  jnp.concatenate([vn_ref[...], pad], axis=0).astype(BF16),
                  preferred_element_type=F32)
    for j in range(n_pages):
        pj = jnp.exp(s_past[j] - m)
        l = l + jnp.sum(pj, axis=-1, keepdims=True)
        v_page = jnp.concatenate([v_refs[j][pl.ds(h, page, stride=HEADS), :]
                                  for h in range(HEADS)], axis=1)
        acc = acc + jnp.dot(pj.astype(BF16), v_page.astype(BF16), preferred_element_type=F32)
    o = acc / l
    lam = _lambda(lam_ref, lam_init)
    for h in range(HEADS):
        lanes = slice(h * HEAD_DIM, (h + 1) * HEAD_DIM)
        r0 = h * 2 * t_new
        o_ref[:, lanes] = _diff_norm(o[r0:r0 + t_new, lanes], o[r0 + t_new:r0 + 2 * t_new, lanes],
                                     lam, g_ref[...], lam_init)


def _attn_sample(proj, cache_k, cache_v, page_table, layer, p, batch, t_new, lam_init):
    n_pages = page_table.shape[1]
    page = cache_k.shape[2]
    ck = cache_k.transpose(0, 1, 3, 4, 5, 2).reshape(cache_k.shape[0], cache_k.shape[1], G, page)
    cv = cache_v.reshape(cache_v.shape[0], cache_v.shape[1], page * HEADS, HEAD_DIM)
    pt = page_table.reshape(-1)

    def new(c):
        return pl.BlockSpec((None, t_new, G), lambda b, pt, c=c: (c, b, 0))

    def paged(j, shape):
        return pl.BlockSpec((None, None) + shape,
                            lambda b, pt, j=j: (layer, pt[b * n_pages + j], 0, 0))

    grid_spec = pltpu.PrefetchScalarGridSpec(
        num_scalar_prefetch=1,
        grid=(batch,),
        in_specs=[new(COL_Q), new(COL_K), new(COL_V),
                  pl.BlockSpec((4, QK_DIM), lambda b, pt: (0, 0)),
                  pl.BlockSpec((1, HEAD_DIM), lambda b, pt: (0, 0))]
                 + [paged(j, (G, page)) for j in range(n_pages)]
                 + [paged(j, (page * HEADS, HEAD_DIM)) for j in range(n_pages)],
        out_specs=pl.BlockSpec((None, t_new, G), lambda b, pt: (b, 0, 0)),
    )
    return pl.pallas_call(
        functools.partial(_attn_sample_kernel, lam_init=lam_init, t_new=t_new,
                          n_pages=n_pages, page=page),
        grid_spec=grid_spec,
        out_shape=jax.ShapeDtypeStruct((batch, t_new, G), F32),
        compiler_params=_params("parallel"),
        name="attn_sample",
    )(pt, proj, proj, proj, p['lam'], p['subln_g'], *([ck] * n_pages), *([cv] * n_pages))


def _outproj_kernel(x_ref, ya_ref, yd_ref, w_ref, g_ref, b_ref, o_ref):
    mix = jnp.dot(ya_ref[...].astype(BF16), w_ref[0:3 * G, :], preferred_element_type=F32)
    mix = mix + jnp.dot(yd_ref[...].astype(BF16), w_ref[3 * G:4 * G, :], preferred_element_type=F32)
    o_ref[...] = _layer_norm(DN_ALPHA * x_ref[...] + mix, g_ref[...], b_ref[...])


def _outproj(x, y_abc, y_d, p, tm):
    n = x.shape[0]
    row = lambda i: (i, 0)
    fixed = lambda i: (0, 0)
    return pl.pallas_call(
        _outproj_kernel,
        grid=(n // tm,),
        in_specs=[pl.BlockSpec((tm, D_MODEL), row), pl.BlockSpec((tm, 3 * G), row),
                  pl.BlockSpec((tm, G), row),
                  pl.BlockSpec((D_MODEL, D_MODEL), fixed, pipeline_mode=pl.Buffered(1)),
                  pl.BlockSpec((1, D_MODEL), fixed), pl.BlockSpec((1, D_MODEL), fixed)],
        out_specs=pl.BlockSpec((tm, D_MODEL), row),
        out_shape=jax.ShapeDtypeStruct((n, D_MODEL), F32),
        compiler_params=_params("parallel"),
        name="outproj_ln",
    )(x, y_abc, y_d, p['w_out'], p['ln1_g'], p['ln1_b'])


def _swiglu_kernel(x_ref, wg_ref, wu_ref, wd_ref, g_ref, b_ref, o_ref, xb_ref, acc_ref):
    j = pl.program_id(1)

    @pl.when(j == 0)
    def _():
        xb_ref[...] = x_ref[...].astype(BF16)
        acc_ref[...] = jnp.zeros_like(acc_ref)

    xb = xb_ref[...]
    gate = jnp.dot(xb, wg_ref[...], preferred_element_type=F32)
    up = jnp.dot(xb, wu_ref[...], preferred_element_type=F32)
    h = gate * _sigmoid(gate) * up
    acc_ref[...] += jnp.dot(h.astype(BF16), wd_ref[...], preferred_element_type=F32)

    @pl.when(j == pl.num_programs(1) - 1)
    def _():
        o_ref[...] = _layer_norm(DN_ALPHA * x_ref[...] + acc_ref[...], g_ref[...], b_ref[...])


def _swiglu(x, p, tm, tf=512):
    n = x.shape[0]
    return pl.pallas_call(
        _swiglu_kernel,
        grid=(n // tm, D_FF // tf),
        in_specs=[pl.BlockSpec((tm, D_MODEL), lambda i, j: (i, 0)),
                  pl.BlockSpec((D_MODEL, tf), lambda i, j: (0, j)),
                  pl.BlockSpec((D_MODEL, tf), lambda i, j: (0, j)),
                  pl.BlockSpec((tf, D_MODEL), lambda i, j: (j, 0)),
                  pl.BlockSpec((1, D_MODEL), lambda i, j: (0, 0)),
                  pl.BlockSpec((1, D_MODEL), lambda i, j: (0, 0))],
        out_specs=pl.BlockSpec((tm, D_MODEL), lambda i, j: (i, 0)),
        out_shape=jax.ShapeDtypeStruct((n, D_MODEL), F32),
        scratch_shapes=[pltpu.VMEM((tm, D_MODEL), BF16), pltpu.VMEM((tm, D_MODEL), F32)],
        compiler_params=_params("parallel", "arbitrary"),
        name="swiglu_ln",
    )(x, p['w_gate'], p['w_up'], p['w_down'], p['ln2_g'], p['ln2_b'])


def kernel(x_prompt, x_sample, cache_k, cache_v, state_conv_a, state_conv_c, page_table, w_in, w_out, conv_a_w, sgu_ln_g, sgu_ln_b, sgu_w, sgu_b, conf_w, conf_b, conf_ln_g, conf_ln_b, lam_q1, lam_k1, lam_q2, lam_k2, subln_g, ln1_g, ln1_b, w_gate, w_up, w_down, ln2_g, ln2_b):
    bp, seq, _ = x_prompt.shape
    bd, t_new, _ = x_sample.shape
    xp = x_prompt.reshape(bp * seq, D_MODEL)
    xs = x_sample.reshape(bd * t_new, D_MODEL)
    outs = [[] for _ in range(9)]
    for l in range(DEPTH):
        row = lambda a: a[l].reshape(1, -1)
        p = {'w_in': w_in[l].astype(BF16), 'w_out': w_out[l].astype(BF16),
             'w_gate': w_gate[l].astype(BF16), 'w_up': w_up[l].astype(BF16),
             'w_down': w_down[l].astype(BF16),
             'conv_a_w': conv_a_w[l], 'sgu_ln_g': row(sgu_ln_g), 'sgu_ln_b': row(sgu_ln_b),
             'sgu_w': sgu_w[l], 'sgu_b': sgu_b[l], 'sgu_b_t': sgu_b[l].T,
             'conf_w': conf_w[l], 'conf_b': row(conf_b),
             'conf_ln_g': row(conf_ln_g), 'conf_ln_b': row(conf_ln_b),
             'lam': jnp.stack([lam_q1[l], lam_k1[l], lam_q2[l], lam_k2[l]]),
             'subln_g': row(subln_g), 'ln1_g': row(ln1_g), 'ln1_b': row(ln1_b),
             'ln2_g': row(ln2_g), 'ln2_b': row(ln2_b)}
        lam_init = 0.8 - 0.6 * math.exp(-0.3 * l)

        proj = _inproj(xp, p['w_in'], tm=1024)
        y_abc, a_p, c_p = _mix_prompt(proj, p, bp, seq)
        y_d = _attn_prompt(proj, p, bp, seq, lam_init)
        xp = _outproj(xp, y_abc.reshape(bp * seq, 3 * G), y_d.reshape(bp * seq, G), p, tm=512)
        xp = _swiglu(xp, p, tm=512)
        k_p = proj[COL_K].reshape(bp, seq, HEADS, 2, QK_DIM)
        v_p = proj[COL_V].reshape(bp, seq, HEADS, HEAD_DIM)

        proj = _inproj(xs, p['w_in'], tm=512)
        y_abc, a_s, c_s, u_s = _mix_sample(proj, state_conv_a[l], state_conv_c[l], p, bd, t_new)
        u_s = u_s.reshape(bd, t_new, G)
        y_d = _attn_sample(proj, cache_k, cache_v, page_table, l, p, bd, t_new, lam_init)
        xs = _outproj(xs, y_abc, y_d.reshape(bd * t_new, G), p, tm=512)
        xs = _swiglu(xs, p, tm=512)
        k_s = proj[COL_K].reshape(bd, t_new, HEADS, 2, QK_DIM)
        v_s = proj[COL_V].reshape(bd, t_new, HEADS, HEAD_DIM)

        for acc, val in zip(outs, (k_p, v_p, k_s, v_s, a_p, a_s, c_p, c_s, u_s)):
            acc.append(val)
    return (xp.reshape(bp, seq, D_MODEL), xs.reshape(bd, t_new, D_MODEL),
            *(jnp.stack(o) for o in outs))
```

```python
import functools
import math

import jax
import jax.numpy as jnp
from jax import lax
from jax.experimental import pallas as pl
from jax.experimental.pallas import tpu as pltpu

D_MODEL = 2048
DEPTH = 2
HEAD_DIM = 128
G = D_MODEL // 4
HEADS = G // HEAD_DIM
QK_DIM = HEAD_DIM // 2
ATT_SCALE = QK_DIM ** -0.5
SC_WIDTH = 3
CONF_WIDTH = 31
CHUNK = 128
D_FF = -(-8 * D_MODEL // (3 * 256)) * 256
DN_ALPHA = (2 * DEPTH) ** 0.25
LN_EPS = 1e-5
NEG_INF = -1e30
LOG2_E = math.log2(math.e)
N_SPLIT = 10
(COL_AB, COL_AC, COL_AH, COL_SU, COL_SV, COL_CA, COL_CG, COL_Q, COL_K, COL_V) = range(N_SPLIT)

F32 = jnp.float32
BF16 = jnp.bfloat16
NT_DIMS = (((1,), (1,)), ((), ()))

VMEM_LIMIT = 56 * 1024 * 1024
HALO = 32
CONV_ROWS = 64


def _params(*sem):
    return pltpu.CompilerParams(dimension_semantics=sem, vmem_limit_bytes=VMEM_LIMIT)


def _layer_norm(x, g, b):
    mu = jnp.mean(x, axis=-1, keepdims=True)
    xc = x - mu
    var = jnp.mean(xc * xc, axis=-1, keepdims=True)
    return xc * lax.rsqrt(var + LN_EPS) * g + b


def _sigmoid(x):
    return 1.0 / (1.0 + jnp.exp(-x))


def _lambda(lam_ref, lam_init):
    lp = lam_ref[...]
    a = jnp.sum(lp[0:1] * lp[1:2], axis=-1, keepdims=True)
    b = jnp.sum(lp[2:3] * lp[3:4], axis=-1, keepdims=True)
    return jnp.exp(a) - jnp.exp(b) + lam_init


def _diff_norm(o0, o1, lam, g, lam_init):
    od = o0 - lam * o1
    ms = jnp.mean(od * od, axis=-1, keepdims=True)
    return od * lax.rsqrt(ms + LN_EPS) * g * (1.0 - lam_init)


def _inproj_kernel(x_ref, w_ref, o_ref, xb_ref):
    @pl.when(pl.program_id(1) == 0)
    def _():
        xb_ref[...] = x_ref[...].astype(BF16)

    o_ref[...] = jnp.dot(xb_ref[...], w_ref[...], preferred_element_type=F32)


def _inproj(x, w, layer, tm):
    n = x.shape[0]
    return pl.pallas_call(
        _inproj_kernel,
        grid=(n // tm, N_SPLIT),
        in_specs=[pl.BlockSpec((tm, D_MODEL), lambda i, j: (i, 0)),
                  pl.BlockSpec((None, D_MODEL, G), lambda i, j: (layer, 0, j))],
        out_specs=pl.BlockSpec((None, tm, G), lambda i, j: (j, i, 0)),
        out_shape=jax.ShapeDtypeStruct((N_SPLIT, n, G), F32),
        scratch_shapes=[pltpu.VMEM((tm, D_MODEL), BF16)],
        compiler_params=_params("parallel", "arbitrary"),
        name="inproj",
    )(x, w)


def _conv31_rows(cw_ref, zx_ref, r0, rows):
    base = HALO - (CONF_WIDTH - 1)
    acc = None
    for r in range(8):
        span = rows + (8 if r else 0)
        part = None
        for j in range(r, base + CONF_WIDTH, 8):
            if j < base:
                continue
            term = cw_ref[j - base:j - base + 1, :] * zx_ref[r0 + j - r:r0 + j - r + span, :]
            part = term if part is None else part + term
        shifted = part[r:r + rows, :]
        acc = shifted if acc is None else acc + shifted
    return acc


def _mix_prompt_kernel(ab_ref, ac_ref, ah_ref, su_ref, sv_ref, ca_ref, cg_ref,
                       hac_ref, hah_ref, hca_ref, hcg_ref,
                       caw_ref, slg_ref, slb_ref, sw_ref, sbt_ref,
                       cw_ref, cb_ref, clg_ref, clb_ref,
                       y_ref, sta_ref, stc_ref,
                       chx_ref, zx_ref, vn_ref, *, tt):
    first = pl.program_id(1) == 0

    ch = ac_ref[...] * ah_ref[...]
    chx_ref[8:8 + tt, :] = ch
    prev = hac_ref[HALO - 8:HALO, :] * hah_ref[HALO - 8:HALO, :]
    chx_ref[0:8, :] = jnp.where(first, 0.0, prev)
    caw = caw_ref[...]
    conv = (caw[0:1] * chx_ref[6:6 + tt, :] + caw[1:2] * chx_ref[7:7 + tt, :]
            + caw[2:3] * chx_ref[8:8 + tt, :])
    y_ref[:, 0:G] = ab_ref[...] * conv
    sta_ref[...] = chx_ref[8 + tt - (SC_WIDTH - 1):8 + tt, :]

    vn_ref[...] = _layer_norm(sv_ref[...], slg_ref[...], slb_ref[...])
    r_i = lax.broadcasted_iota(jnp.int32, (CHUNK, CHUNK), 0)
    c_i = lax.broadcasted_iota(jnp.int32, (CHUNK, CHUNK), 1)
    for h in range(HEADS):
        lanes = slice(h * HEAD_DIM, (h + 1) * HEAD_DIM)
        w_c = jnp.where(c_i <= r_i, sw_ref[h], 0.0).astype(BF16)
        bias = sbt_ref[:, h:h + 1]
        for n in range(tt // CHUNK):
            rows = slice(n * CHUNK, (n + 1) * CHUNK)
            mixed = jnp.dot(w_c, vn_ref[rows, lanes].astype(BF16),
                            preferred_element_type=F32) + bias
            y_ref[rows, G + h * HEAD_DIM:G + (h + 1) * HEAD_DIM] = su_ref[rows, lanes] * mixed

    zx_ref[HALO:HALO + tt, :] = ca_ref[...] * _sigmoid(cg_ref[...])
    zprev = hca_ref[...] * _sigmoid(hcg_ref[...])
    zx_ref[0:HALO, :] = jnp.where(first, 0.0, zprev)
    stc_ref[...] = zx_ref[HALO + tt - (CONF_WIDTH - 1):HALO + tt, :]
    for c in range(tt // CONV_ROWS):
        r0 = c * CONV_ROWS
        zc = _conv31_rows(cw_ref, zx_ref, r0, CONV_ROWS)
        zn = _layer_norm(zc + cb_ref[...], clg_ref[...], clb_ref[...])
        y_ref[r0:r0 + CONV_ROWS, 2 * G:3 * G] = zn * _sigmoid(zn)


def _mix_prompt(proj, p, batch, seq, tt=256):
    proj4 = proj.reshape(N_SPLIT, batch, seq, G)
    hb = tt // HALO

    def col(c):
        return pl.BlockSpec((None, None, tt, G), lambda b, t, c=c: (c, b, t, 0))

    def halo(c):
        return pl.BlockSpec((None, None, HALO, G),
                            lambda b, t, c=c: (c, b, jnp.maximum(t * hb - 1, 0), 0))

    def full(a):
        return pl.BlockSpec(a.shape, lambda b, t, nd=a.ndim: (0,) * nd)

    weights = [p['conv_a_w'], p['sgu_ln_g'], p['sgu_ln_b'], p['sgu_w'], p['sgu_b_t'],
               p['conf_w'], p['conf_b'], p['conf_ln_g'], p['conf_ln_b']]
    return pl.pallas_call(
        functools.partial(_mix_prompt_kernel, tt=tt),
        grid=(batch, seq // tt),
        in_specs=[col(c) for c in (COL_AB, COL_AC, COL_AH, COL_SU, COL_SV, COL_CA, COL_CG)]
                 + [halo(c) for c in (COL_AC, COL_AH, COL_CA, COL_CG)]
                 + [full(a) for a in weights],
        out_specs=[pl.BlockSpec((None, tt, 3 * G), lambda b, t: (b, t, 0)),
                   pl.BlockSpec((None, SC_WIDTH - 1, G), lambda b, t: (b, 0, 0)),
                   pl.BlockSpec((None, CONF_WIDTH - 1, G), lambda b, t: (b, 0, 0))],
        out_shape=[jax.ShapeDtypeStruct((batch, seq, 3 * G), F32),
                   jax.ShapeDtypeStruct((batch, SC_WIDTH - 1, G), F32),
                   jax.ShapeDtypeStruct((batch, CONF_WIDTH - 1, G), F32)],
        scratch_shapes=[pltpu.VMEM((8 + tt, G), F32), pltpu.VMEM((HALO + tt, G), F32),
                        pltpu.VMEM((tt, G), F32)],
        compiler_params=_params("parallel", "arbitrary"),
        name="mix_prompt",
    )(*([proj4] * 11), *weights)


def _mix_sample_kernel(ab_ref, ac_ref, ah_ref, su_ref, sv_ref, ca_ref, cg_ref,
                       ha_ref, hc_ref,
                       caw_ref, slg_ref, slb_ref, swd_ref, sbc_ref,
                       cw_ref, cb_ref, clg_ref, clb_ref,
                       y_ref, sta_ref, stc_ref, vn_ref,
                       ch_ref, z_ref, zc_ref, chx_ref, zx_ref, *, bs, t_new):
    ka, kc = SC_WIDTH - 1, CONF_WIDTH - 1

    ch_ref[...] = ac_ref[...] * ah_ref[...]
    z_ref[...] = ca_ref[...] * _sigmoid(cg_ref[...])
    vn_ref[...] = _layer_norm(sv_ref[...], slg_ref[...], slb_ref[...])

    for h in range(HEADS):
        lanes = slice(h * HEAD_DIM, (h + 1) * HEAD_DIM)
        mixed = jnp.dot(swd_ref[h].astype(BF16), vn_ref[:, lanes].astype(BF16),
                        preferred_element_type=F32) + sbc_ref[:, h:h + 1]
        y_ref[:, G + h * HEAD_DIM:G + (h + 1) * HEAD_DIM] = su_ref[:, lanes] * mixed

    caw = caw_ref[...]

    def per_sample(b, carry):
        r = pl.multiple_of(b * t_new, t_new)
        new = pl.ds(r, t_new)
        chx_ref[8 - ka:8, :] = ha_ref[b]
        chx_ref[8:8 + t_new, :] = ch_ref[new, :]
        conv = (caw[0:1] * chx_ref[6:6 + t_new, :] + caw[1:2] * chx_ref[7:7 + t_new, :]
                + caw[2:3] * chx_ref[8:8 + t_new, :])
        y_ref[new, 0:G] = ab_ref[new, :] * conv
        sta_ref[b] = chx_ref[8 + t_new - ka:8 + t_new, :]

        zx_ref[HALO - kc:HALO, :] = hc_ref[b]
        zx_ref[HALO:HALO + t_new, :] = z_ref[new, :]
        acc = None
        for k in range(CONF_WIDTH):
            term = cw_ref[k:k + 1, :] * zx_ref[HALO - kc + k:HALO - kc + k + t_new, :]
            acc = term if acc is None else acc + term
        zc_ref[new, :] = acc
        stc_ref[b] = zx_ref[HALO + t_new - kc:HALO + t_new, :]
        return carry

    lax.fori_loop(0, bs, per_sample, 0)
    zn = _layer_norm(zc_ref[...] + cb_ref[...], clg_ref[...], clb_ref[...])
    y_ref[:, 2 * G:3 * G] = zn * _sigmoid(zn)


def _mix_sample(proj, hist_a, hist_c, p, batch, t_new, bs=16):
    ka, kc = SC_WIDTH - 1, CONF_WIDTH - 1
    n = batch * t_new
    rows = bs * t_new
    tril = jnp.tril(jnp.ones((t_new, t_new), F32))
    w8 = p['sgu_w'][:, :t_new, :t_new] * tril
    swd = jax.vmap(lambda w: jnp.kron(jnp.eye(bs, dtype=F32), w))(w8)
    sbc = jnp.tile(p['sgu_b'][:, :t_new], (1, bs)).T

    def col(c):
        return pl.BlockSpec((None, rows, G), lambda i, c=c: (c, i, 0))

    def full(a):
        return pl.BlockSpec(a.shape, lambda i, nd=a.ndim: (0,) * nd)

    weights = [p['conv_a_w'], p['sgu_ln_g'], p['sgu_ln_b'], swd, sbc,
               p['conf_w'], p['conf_b'], p['conf_ln_g'], p['conf_ln_b']]
    return pl.pallas_call(
        functools.partial(_mix_sample_kernel, bs=bs, t_new=t_new),
        grid=(batch // bs,),
        in_specs=[col(c) for c in (COL_AB, COL_AC, COL_AH, COL_SU, COL_SV, COL_CA, COL_CG)]
                 + [pl.BlockSpec((bs, ka, G), lambda i: (i, 0, 0)),
                    pl.BlockSpec((bs, kc, G), lambda i: (i, 0, 0))]
                 + [full(a) for a in weights],
        out_specs=[pl.BlockSpec((rows, 3 * G), lambda i: (i, 0)),
                   pl.BlockSpec((bs, ka, G), lambda i: (i, 0, 0)),
                   pl.BlockSpec((bs, kc, G), lambda i: (i, 0, 0)),
                   pl.BlockSpec((rows, G), lambda i: (i, 0))],
        out_shape=[jax.ShapeDtypeStruct((n, 3 * G), F32),
                   jax.ShapeDtypeStruct((batch, ka, G), F32),
                   jax.ShapeDtypeStruct((batch, kc, G), F32),
                   jax.ShapeDtypeStruct((n, G), F32)],
        scratch_shapes=[pltpu.VMEM((rows, G), F32)] * 3
                       + [pltpu.VMEM((8 + t_new, G), F32), pltpu.VMEM((HALO + t_new, G), F32)],
        compiler_params=_params("parallel"),
        name="mix_sample",
    )(*([proj] * 7), hist_a, hist_c, *weights)


def _attn_prompt_kernel(q_ref, k_ref, v_ref, lam_ref, g_ref, o_ref,
                        kb_ref, vb_ref, q2_ref, m_ref, l_ref, acc_ref, *, lam_init, tq):
    qi = pl.program_id(2)

    @pl.when(qi == 0)
    def _():
        kb_ref[...] = k_ref[...].astype(BF16)
        vb_ref[...] = v_ref[...].astype(BF16)

    th = tq // 2
    lane = lax.broadcasted_iota(jnp.int32, (th, HEAD_DIM), 1)
    for u in range(2):
        q = q_ref[u * th:(u + 1) * th, :] * (ATT_SCALE * LOG2_E)
        q2_ref[u, 0:th, :] = jnp.where(lane < QK_DIM, q, 0.0).astype(BF16)
        q2_ref[u, th:2 * th, :] = jnp.where(lane >= QK_DIM, q, 0.0).astype(BF16)
    def step(u, key0, n_keys, triangular=False, first=False):
        keys = pl.ds(key0, n_keys)
        s = lax.dot_general(q2_ref[u], kb_ref[keys, :], NT_DIMS, preferred_element_type=F32)
        if triangular:
            row = lax.broadcasted_iota(jnp.int32, (2 * th, n_keys), 0) % th
            col = lax.broadcasted_iota(jnp.int32, (2 * th, n_keys), 1)
            s = jnp.where(col <= row, s, NEG_INF)
        m_cur = jnp.broadcast_to(jnp.max(s, axis=-1, keepdims=True), (2 * th, HEAD_DIM))
        m_new = m_cur if first else jnp.maximum(m_ref[u], m_cur)
        p = jnp.exp2(s - jnp.concatenate([m_new] * (n_keys // HEAD_DIM), axis=1))
        l_cur = jnp.broadcast_to(jnp.sum(p, axis=-1, keepdims=True), (2 * th, HEAD_DIM))
        pv = jnp.dot(p.astype(BF16), vb_ref[keys, :], preferred_element_type=F32)
        if first:
            l_ref[u] = l_cur
            acc_ref[u] = pv
        else:
            alpha = jnp.exp2(m_ref[u] - m_new)
            l_ref[u] = alpha * l_ref[u] + l_cur
            acc_ref[u] = alpha * acc_ref[u] + pv
        m_ref[u] = m_new

    diag = pl.multiple_of(qi * tq, tq)
    step(0, diag, th, triangular=True, first=True)
    step(1, diag, th, first=True)
    step(1, diag + th, th, triangular=True)

    def full_block(j, carry):
        key0 = pl.multiple_of(j * tq, tq)
        step(0, key0, tq, False)
        step(1, key0, tq, False)
        return carry

    lax.fori_loop(0, qi, full_block, 0)
    lam = _lambda(lam_ref, lam_init)
    for u in range(2):
        o = acc_ref[u] / l_ref[u]
        o_ref[u * th:(u + 1) * th, :] = _diff_norm(o[:th], o[th:], lam, g_ref[...], lam_init)


def _attn_prompt(proj, p, batch, seq, lam_init, tq=512):
    proj4 = proj.reshape(N_SPLIT, batch, seq, G)

    def kv(c):
        return pl.BlockSpec((None, None, seq, HEAD_DIM), lambda b, h, i, c=c: (c, b, 0, h))

    return pl.pallas_call(
        functools.partial(_attn_prompt_kernel, lam_init=lam_init, tq=tq),
        grid=(batch, HEADS, seq // tq),
        in_specs=[pl.BlockSpec((None, None, tq, HEAD_DIM), lambda b, h, i: (COL_Q, b, i, h)),
                  kv(COL_K), kv(COL_V),
                  pl.BlockSpec((4, QK_DIM), lambda b, h, i: (0, 0)),
                  pl.BlockSpec((1, HEAD_DIM), lambda b, h, i: (0, 0))],
        out_specs=pl.BlockSpec((None, tq, HEAD_DIM), lambda b, h, i: (b, i, h)),
        out_shape=jax.ShapeDtypeStruct((batch, seq, G), F32),
        scratch_shapes=[pltpu.VMEM((seq, HEAD_DIM), BF16)] * 2
                       + [pltpu.VMEM((2, tq, HEAD_DIM), BF16)]
                       + [pltpu.VMEM((2, tq, HEAD_DIM), F32)] * 3,
        compiler_params=_params("parallel", "parallel", "arbitrary"),
        name="attn_prompt",
    )(proj4, proj4, proj4, p['lam'], p['subln_g'])


def _attn_sample_kernel(pt_ref, q_ref, kn_ref, vn_ref, lam_ref, g_ref, *rest,
                        lam_init, t_new, n_pages, page, ns):
    del pt_ref
    o_ref = rest[2 * ns * n_pages]
    lam = _lambda(lam_ref, lam_init)
    for s in range(ns):
        rows = slice(s * t_new, (s + 1) * t_new)
        _attn_one_sample(q_ref[rows, :], kn_ref[rows, :], vn_ref[rows, :], lam, g_ref[...],
                         rest[s * n_pages:(s + 1) * n_pages],
                         rest[(ns + s) * n_pages:(ns + s + 1) * n_pages],
                         o_ref.at[s], lam_init, t_new, n_pages, page)


def _attn_one_sample(q, k_new, v_new, lam, gain, k_refs, v_refs, o_ref,
                     lam_init, t_new, n_pages, page):
    n_rows = HEADS * 2 * t_new

    q = q * ATT_SCALE
    qt = jnp.concatenate([q] * (HEADS * 2), axis=0)
    r_i = lax.broadcasted_iota(jnp.int32, (n_rows, G), 0)
    c_i = lax.broadcasted_iota(jnp.int32, (n_rows, G), 1)
    qbig = jnp.where(c_i // QK_DIM == r_i // t_new, qt, 0.0).astype(BF16)

    s_past = [jnp.dot(qbig, k_refs[j][...].astype(BF16), preferred_element_type=F32)
              for j in range(n_pages)]
    pad = jnp.zeros((page - t_new, G), F32)
    s_new = lax.dot_general(qbig, jnp.concatenate([k_new, pad], axis=0).astype(BF16),
                            NT_DIMS, preferred_element_type=F32)
    r_n = lax.broadcasted_iota(jnp.int32, (n_rows, page), 0) % t_new
    c_n = lax.broadcasted_iota(jnp.int32, (n_rows, page), 1)
    s_new = jnp.where(c_n <= r_n, s_new, NEG_INF)

    m = s_new
    for s in s_past:
        m = jnp.maximum(m, s)
    m = jnp.max(m, axis=-1, keepdims=True)
    p_new = jnp.exp(s_new - m)
    l = jnp.sum(p_new, axis=-1, keepdims=True)
    acc = jnp.dot(p_new.astype(BF16), jnp.concatenate([v_new, pad], axis=0).astype(BF16),
                  preferred_element_type=F32)
    for j in range(n_pages):
        pj = jnp.exp(s_past[j] - m)
        l = l + jnp.sum(pj, axis=-1, keepdims=True)
        v_page = jnp.concatenate([v_refs[j][pl.ds(h, page, stride=HEADS), :]
                                  for h in range(HEADS)], axis=1)
        acc = acc + jnp.dot(pj.astype(BF16), v_page.astype(BF16), preferred_element_type=F32)
    o = acc / l
    for h in range(HEADS):
        lanes = slice(h * HEAD_DIM, (h + 1) * HEAD_DIM)
        r0 = h * 2 * t_new
        o_ref[:, lanes] = _diff_norm(o[r0:r0 + t_new, lanes], o[r0 + t_new:r0 + 2 * t_new, lanes],
                                     lam, gain, lam_init)


def _attn_sample(proj, cache_k, cache_v, page_table, layer, p, batch, t_new, lam_init, ns=2):
    n_pages = page_table.shape[1]
    page = cache_k.shape[2]
    ck = cache_k.transpose(0, 1, 3, 4, 5, 2).reshape(cache_k.shape[0], cache_k.shape[1], G, page)
    cv = cache_v.reshape(cache_v.shape[0], cache_v.shape[1], page * HEADS, HEAD_DIM)
    pt = page_table.reshape(-1)

    def new(c):
        return pl.BlockSpec((None, ns * t_new, G), lambda b, pt, c=c: (c, b, 0))

    def paged(j, shape):
        return pl.BlockSpec((None, None) + shape,
                            lambda b, pt, j=j: (layer, pt[b * ns * n_pages + j], 0, 0))

    n_in = ns * n_pages
    grid_spec = pltpu.PrefetchScalarGridSpec(
        num_scalar_prefetch=1,
        grid=(batch // ns,),
        in_specs=[new(COL_Q), new(COL_K), new(COL_V),
                  pl.BlockSpec((4, QK_DIM), lambda b, pt: (0, 0)),
                  pl.BlockSpec((1, HEAD_DIM), lambda b, pt: (0, 0))]
                 + [paged(j, (G, page)) for j in range(n_in)]
                 + [paged(j, (page * HEADS, HEAD_DIM)) for j in range(n_in)],
        out_specs=pl.BlockSpec((ns, t_new, G), lambda b, pt: (b, 0, 0)),
    )
    return pl.pallas_call(
        functools.partial(_attn_sample_kernel, lam_init=lam_init, t_new=t_new,
                          n_pages=n_pages, page=page, ns=ns),
        grid_spec=grid_spec,
        out_shape=jax.ShapeDtypeStruct((batch, t_new, G), F32),
        compiler_params=_params("parallel"),
        name="attn_sample",
    )(pt, proj, proj, proj, p['lam'], p['subln_g'], *([ck] * n_in), *([cv] * n_in))


def _outproj_kernel(x_ref, ya_ref, yd_ref, w_ref, g_ref, b_ref, o_ref):
    mix = jnp.dot(ya_ref[...].astype(BF16), w_ref[0:3 * G, :], preferred_element_type=F32)
    mix = mix + jnp.dot(yd_ref[...].astype(BF16), w_ref[3 * G:4 * G, :], preferred_element_type=F32)
    o_ref[...] = _layer_norm(DN_ALPHA * x_ref[...] + mix, g_ref[...], b_ref[...])


def _outproj(x, y_abc, y_d, p, layer, tm):
    n = x.shape[0]
    row = lambda i: (i, 0)
    fixed = lambda i: (0, 0)
    return pl.pallas_call(
        _outproj_kernel,
        grid=(n // tm,),
        in_specs=[pl.BlockSpec((tm, D_MODEL), row), pl.BlockSpec((tm, 3 * G), row),
                  pl.BlockSpec((tm, G), row),
                  pl.BlockSpec((None, D_MODEL, D_MODEL), lambda i: (layer, 0, 0),
                               pipeline_mode=pl.Buffered(1)),
                  pl.BlockSpec((1, D_MODEL), fixed), pl.BlockSpec((1, D_MODEL), fixed)],
        out_specs=pl.BlockSpec((tm, D_MODEL), row),
        out_shape=jax.ShapeDtypeStruct((n, D_MODEL), F32),
        compiler_params=_params("parallel"),
        name="outproj_ln",
    )(x, y_abc, y_d, p['w_out'], p['ln1_g'], p['ln1_b'])


def _swiglu_kernel(x_ref, wg_ref, wu_ref, wd_ref, g_ref, b_ref, o_ref, xb_ref):
    j = pl.program_id(1)

    @pl.when(j == 0)
    def _():
        xb_ref[...] = x_ref[...].astype(BF16)
        o_ref[...] = jnp.zeros_like(o_ref)

    xb = xb_ref[...]
    gate = jnp.dot(xb, wg_ref[...], preferred_element_type=F32)
    up = jnp.dot(xb, wu_ref[...], preferred_element_type=F32)
    h = gate * _sigmoid(gate) * up
    o_ref[...] += jnp.dot(h.astype(BF16), wd_ref[...], preferred_element_type=F32)

    @pl.when(j == pl.num_programs(1) - 1)
    def _():
        o_ref[...] = _layer_norm(DN_ALPHA * x_ref[...] + o_ref[...], g_ref[...], b_ref[...])


def _swiglu(x, p, layer, tm, tf=512):
    n = x.shape[0]
    return pl.pallas_call(
        _swiglu_kernel,
        grid=(n // tm, D_FF // tf),
        in_specs=[pl.BlockSpec((tm, D_MODEL), lambda i, j: (i, 0)),
                  pl.BlockSpec((None, D_MODEL, tf), lambda i, j: (layer, 0, j)),
                  pl.BlockSpec((None, D_MODEL, tf), lambda i, j: (layer, 0, j)),
                  pl.BlockSpec((None, tf, D_MODEL), lambda i, j: (layer, j, 0)),
                  pl.BlockSpec((1, D_MODEL), lambda i, j: (0, 0)),
                  pl.BlockSpec((1, D_MODEL), lambda i, j: (0, 0))],
        out_specs=pl.BlockSpec((tm, D_MODEL), lambda i, j: (i, 0)),
        out_shape=jax.ShapeDtypeStruct((n, D_MODEL), F32),
        scratch_shapes=[pltpu.VMEM((tm, D_MODEL), BF16)],
        compiler_params=_params("parallel", "arbitrary"),
        name="swiglu_ln",
    )(x, p['w_gate'], p['w_up'], p['w_down'], p['ln2_g'], p['ln2_b'])


def kernel(x_prompt, x_sample, cache_k, cache_v, state_conv_a, state_conv_c, page_table, w_in, w_out, conv_a_w, sgu_ln_g, sgu_ln_b, sgu_w, sgu_b, conf_w, conf_b, conf_ln_g, conf_ln_b, lam_q1, lam_k1, lam_q2, lam_k2, subln_g, ln1_g, ln1_b, w_gate, w_up, w_down, ln2_g, ln2_b):
    bp, seq, _ = x_prompt.shape
    bd, t_new, _ = x_sample.shape
    xp = x_prompt.reshape(bp * seq, D_MODEL)
    xs = x_sample.reshape(bd * t_new, D_MODEL)
    outs = [[] for _ in range(9)]
    dense = {'w_in': w_in.astype(BF16), 'w_out': w_out.astype(BF16), 'w_gate': w_gate.astype(BF16),
             'w_up': w_up.astype(BF16), 'w_down': w_down.astype(BF16)}
    for l in range(DEPTH):
        row = lambda a: a[l].reshape(1, -1)
        p = {**dense,
             'conv_a_w': conv_a_w[l], 'sgu_ln_g': row(sgu_ln_g), 'sgu_ln_b': row(sgu_ln_b),
             'sgu_w': sgu_w[l], 'sgu_b': sgu_b[l], 'sgu_b_t': sgu_b[l].T,
             'conf_w': conf_w[l], 'conf_b': row(conf_b),
             'conf_ln_g': row(conf_ln_g), 'conf_ln_b': row(conf_ln_b),
             'lam': jnp.stack([lam_q1[l], lam_k1[l], lam_q2[l], lam_k2[l]]),
             'subln_g': row(subln_g), 'ln1_g': row(ln1_g), 'ln1_b': row(ln1_b),
             'ln2_g': row(ln2_g), 'ln2_b': row(ln2_b)}
        lam_init = 0.8 - 0.6 * math.exp(-0.3 * l)

        proj = _inproj(xp, p['w_in'], l, tm=1024)
        y_abc, a_p, c_p = _mix_prompt(proj, p, bp, seq)
        y_d = _attn_prompt(proj, p, bp, seq, lam_init)
        xp = _outproj(xp, y_abc.reshape(bp * seq, 3 * G), y_d.reshape(bp * seq, G), p, l, tm=512)
        xp = _swiglu(xp, p, l, tm=512)
        k_p = proj[COL_K].reshape(bp, seq, HEADS, 2, QK_DIM)
        v_p = proj[COL_V].reshape(bp, seq, HEADS, HEAD_DIM)

        proj = _inproj(xs, p['w_in'], l, tm=512)
        y_abc, a_s, c_s, u_s = _mix_sample(proj, state_conv_a[l], state_conv_c[l], p, bd, t_new)
        u_s = u_s.reshape(bd, t_new, G)
        y_d = _attn_sample(proj, cache_k, cache_v, page_table, l, p, bd, t_new, lam_init)
        xs = _outproj(xs, y_abc, y_d.reshape(bd * t_new, G), p, l, tm=512)
        xs = _swiglu(xs, p, l, tm=512)
        k_s = proj[COL_K].reshape(bd, t_new, HEADS, 2, QK_DIM)
        v_s = proj[COL_V].reshape(bd, t_new, HEADS, HEAD_DIM)

        for acc, val in zip(outs, (k_p, v_p, k_s, v_s, a_p, a_s, c_p, c_s, u_s)):
            acc.append(val)
    return (xp.reshape(bp, seq, D_MODEL), xs.reshape(bd, t_new, D_MODEL),
            *(jnp.stack(o) for o in outs))
```

```python
import functools
import math

import jax
import jax.numpy as jnp
from jax import lax
from jax.experimental import pallas as pl
from jax.experimental.pallas import tpu as pltpu

D_MODEL = 2048
DEPTH = 2
HEAD_DIM = 128
G = D_MODEL // 4
HEADS = G // HEAD_DIM
QK_DIM = HEAD_DIM // 2
ATT_SCALE = QK_DIM ** -0.5
SC_WIDTH = 3
CONF_WIDTH = 31
CHUNK = 128
D_FF = -(-8 * D_MODEL // (3 * 256)) * 256
DN_ALPHA = (2 * DEPTH) ** 0.25
LN_EPS = 1e-5
NEG_INF = -1e30
LOG2_E = math.log2(math.e)
N_SPLIT = 10
N_MAIN = 8
(COL_AB, COL_AC, COL_AH, COL_SU, COL_SV, COL_CA, COL_CG, COL_Q, COL_K, COL_V) = range(N_SPLIT)

F32 = jnp.float32
BF16 = jnp.bfloat16
NT_DIMS = (((1,), (1,)), ((), ()))

VMEM_LIMIT = 60 * 1024 * 1024
HALO = 32
CONV_ROWS = 64


def _params(*sem):
    return pltpu.CompilerParams(dimension_semantics=sem, vmem_limit_bytes=VMEM_LIMIT)


def _layer_norm(x, g, b):
    mu = jnp.mean(x, axis=-1, keepdims=True)
    xc = x - mu
    var = jnp.mean(xc * xc, axis=-1, keepdims=True)
    return xc * lax.rsqrt(var + LN_EPS) * g + b


def _sigmoid(x):
    return 1.0 / (1.0 + jnp.exp(-x))


def _lambda(lam_ref, lam_init):
    lp = lam_ref[...]
    a = jnp.sum(lp[0:1] * lp[1:2], axis=-1, keepdims=True)
    b = jnp.sum(lp[2:3] * lp[3:4], axis=-1, keepdims=True)
    return jnp.exp(a) - jnp.exp(b) + lam_init


def _diff_norm(o0, o1, lam, g, lam_init):
    od = o0 - lam * o1
    ms = jnp.mean(od * od, axis=-1, keepdims=True)
    return od * lax.rsqrt(ms + LN_EPS) * g * (1.0 - lam_init)


def _inproj_kernel(x_ref, w_ref, o_ref, xb_ref):
    @pl.when(pl.program_id(1) == 0)
    def _():
        xb_ref[...] = x_ref[...].astype(BF16)

    r = jnp.dot(xb_ref[...], w_ref[...], preferred_element_type=F32)
    o_ref[0] = r[:, 0:G]
    o_ref[1] = r[:, G:2 * G]


def _inproj(x, w, layer, tm):
    n = x.shape[0]
    return pl.pallas_call(
        _inproj_kernel,
        grid=(n // tm, N_SPLIT // 2),
        in_specs=[pl.BlockSpec((tm, D_MODEL), lambda i, j: (i, 0)),
                  pl.BlockSpec((None, D_MODEL, 2 * G), lambda i, j: (layer, 0, j))],
        out_specs=pl.BlockSpec((2, tm, G), lambda i, j: (j, i, 0)),
        out_shape=jax.ShapeDtypeStruct((N_SPLIT, n, G), F32),
        scratch_shapes=[pltpu.VMEM((tm, D_MODEL), BF16)],
        compiler_params=_params("parallel", "arbitrary"),
        name="inproj",
    )(x, w)


def _inproj_prompt_kernel(x_ref, w_ref, wkt_ref, o_ref, kt_ref, v_ref, vn_ref, xb_ref):
    j = pl.program_id(1)
    tm = x_ref.shape[0]

    @pl.when(j == 0)
    def _():
        xb_ref[...] = x_ref[...].astype(BF16)

    @pl.when(j < N_MAIN // 2)
    def _():
        r = jnp.dot(xb_ref[...], w_ref[...], preferred_element_type=F32)
        o_ref[0] = r[:, 0:G]
        o_ref[1] = r[:, G:2 * G]

    @pl.when(j == N_MAIN // 2)
    def _():
        kt_ref[...] = lax.dot_general(wkt_ref[...], xb_ref[...], NT_DIMS,
                                      preferred_element_type=F32)
        v = jnp.dot(xb_ref[...], w_ref[:, G:2 * G], preferred_element_type=F32)
        v_ref[...] = v
        for h in range(HEADS):
            vn_ref[pl.ds(h, tm, stride=HEADS), :] = v[:, h * HEAD_DIM:(h + 1) * HEAD_DIM]


def _inproj_prompt(x, w, wkt, layer, batch, seq, tm):
    n = x.shape[0]
    nt = seq // tm
    last = N_MAIN // 2 - 1
    return pl.pallas_call(
        _inproj_prompt_kernel,
        grid=(n // tm, N_MAIN // 2 + 1),
        in_specs=[pl.BlockSpec((tm, D_MODEL), lambda i, j: (i, 0)),
                  pl.BlockSpec((None, D_MODEL, 2 * G), lambda i, j: (layer, 0, j)),
                  pl.BlockSpec((None, G, D_MODEL), lambda i, j: (layer, 0, 0),
                               pipeline_mode=pl.Buffered(1))],
        out_specs=[pl.BlockSpec((2, tm, G), lambda i, j: (jnp.minimum(j, last), i, 0)),
                   pl.BlockSpec((None, G, tm), lambda i, j: (i // nt, 0, i % nt)),
                   pl.BlockSpec((tm, G), lambda i, j: (i, 0)),
                   pl.BlockSpec((tm * HEADS, HEAD_DIM), lambda i, j: (i, 0))],
        out_shape=[jax.ShapeDtypeStruct((N_MAIN, n, G), F32),
                   jax.ShapeDtypeStruct((batch, G, seq), F32),
                   jax.ShapeDtypeStruct((n, G), F32),
                   jax.ShapeDtypeStruct((n * HEADS, HEAD_DIM), F32)],
        scratch_shapes=[pltpu.VMEM((tm, D_MODEL), BF16)],
        compiler_params=_params("parallel", "arbitrary"),
        name="inproj_prompt",
    )(x, w, wkt)


def _conv31_rows(cw_ref, zx_ref, r0, rows):
    base = HALO - (CONF_WIDTH - 1)
    acc = None
    for r in range(8):
        span = rows + (8 if r else 0)
        part = None
        for j in range(r, base + CONF_WIDTH, 8):
            if j < base:
                continue
            term = cw_ref[j - base:j - base + 1, :] * zx_ref[r0 + j - r:r0 + j - r + span, :]
            part = term if part is None else part + term
        shifted = part[r:r + rows, :]
        acc = shifted if acc is None else acc + shifted
    return acc


def _mix_prompt_kernel(ab_ref, ac_ref, ah_ref, su_ref, sv_ref, ca_ref, cg_ref,
                       hac_ref, hah_ref, hca_ref, hcg_ref,
                       caw_ref, slg_ref, slb_ref, sw_ref, sbt_ref,
                       cw_ref, cb_ref, clg_ref, clb_ref,
                       y_ref, sta_ref, stc_ref,
                       chx_ref, zx_ref, vn_ref, *, tt):
    first = pl.program_id(1) == 0

    ch = ac_ref[...] * ah_ref[...]
    chx_ref[8:8 + tt, :] = ch
    prev = hac_ref[HALO - 8:HALO, :] * hah_ref[HALO - 8:HALO, :]
    chx_ref[0:8, :] = jnp.where(first, 0.0, prev)
    caw = caw_ref[...]
    conv = (caw[0:1] * chx_ref[6:6 + tt, :] + caw[1:2] * chx_ref[7:7 + tt, :]
            + caw[2:3] * chx_ref[8:8 + tt, :])
    y_ref[:, 0:G] = (ab_ref[...] * conv).astype(y_ref.dtype)
    sta_ref[...] = chx_ref[8 + tt - (SC_WIDTH - 1):8 + tt, :]

    vn_ref[...] = _layer_norm(sv_ref[...], slg_ref[...], slb_ref[...])
    r_i = lax.broadcasted_iota(jnp.int32, (CHUNK, CHUNK), 0)
    c_i = lax.broadcasted_iota(jnp.int32, (CHUNK, CHUNK), 1)
    for h in range(HEADS):
        lanes = slice(h * HEAD_DIM, (h + 1) * HEAD_DIM)
        w_c = jnp.where(c_i <= r_i, sw_ref[h], 0.0).astype(BF16)
        bias = sbt_ref[:, h:h + 1]
        for n in range(tt // CHUNK):
            rows = slice(n * CHUNK, (n + 1) * CHUNK)
            mixed = jnp.dot(w_c, vn_ref[rows, lanes].astype(BF16),
                            preferred_element_type=F32) + bias
            y_ref[rows, G + h * HEAD_DIM:G + (h + 1) * HEAD_DIM] = (
                su_ref[rows, lanes] * mixed).astype(y_ref.dtype)

    zx_ref[HALO:HALO + tt, :] = ca_ref[...] * _sigmoid(cg_ref[...])
    zprev = hca_ref[...] * _sigmoid(hcg_ref[...])
    zx_ref[0:HALO, :] = jnp.where(first, 0.0, zprev)
    stc_ref[...] = zx_ref[HALO + tt - (CONF_WIDTH - 1):HALO + tt, :]
    for c in range(tt // CONV_ROWS):
        r0 = c * CONV_ROWS
        zc = _conv31_rows(cw_ref, zx_ref, r0, CONV_ROWS)
        zn = _layer_norm(zc + cb_ref[...], clg_ref[...], clb_ref[...])
        y_ref[r0:r0 + CONV_ROWS, 2 * G:3 * G] = (zn * _sigmoid(zn)).astype(y_ref.dtype)


def _mix_prompt(proj, p, batch, seq, tt=256):
    proj4 = proj.reshape(proj.shape[0], batch, seq, G)
    hb = tt // HALO

    def col(c):
        return pl.BlockSpec((None, None, tt, G), lambda b, t, c=c: (c, b, t, 0))

    def halo(c):
        return pl.BlockSpec((None, None, HALO, G),
                            lambda b, t, c=c: (c, b, jnp.maximum(t * hb - 1, 0), 0))

    def full(a):
        return pl.BlockSpec(a.shape, lambda b, t, nd=a.ndim: (0,) * nd)

    weights = [p['conv_a_w'], p['sgu_ln_g'], p['sgu_ln_b'], p['sgu_w'], p['sgu_b_t'],
               p['conf_w'], p['conf_b'], p['conf_ln_g'], p['conf_ln_b']]
    return pl.pallas_call(
        functools.partial(_mix_prompt_kernel, tt=tt),
        grid=(batch, seq // tt),
        in_specs=[col(c) for c in (COL_AB, COL_AC, COL_AH, COL_SU, COL_SV, COL_CA, COL_CG)]
                 + [halo(c) for c in (COL_AC, COL_AH, COL_CA, COL_CG)]
                 + [full(a) for a in weights],
        out_specs=[pl.BlockSpec((None, tt, 3 * G), lambda b, t: (b, t, 0)),
                   pl.BlockSpec((None, SC_WIDTH - 1, G), lambda b, t: (b, 0, 0)),
                   pl.BlockSpec((None, CONF_WIDTH - 1, G), lambda b, t: (b, 0, 0))],
        out_shape=[jax.ShapeDtypeStruct((batch, seq, 3 * G), BF16),
                   jax.ShapeDtypeStruct((batch, SC_WIDTH - 1, G), F32),
                   jax.ShapeDtypeStruct((batch, CONF_WIDTH - 1, G), F32)],
        scratch_shapes=[pltpu.VMEM((8 + tt, G), F32), pltpu.VMEM((HALO + tt, G), F32),
                        pltpu.VMEM((tt, G), F32)],
        compiler_params=_params("parallel", "arbitrary"),
        name="mix_prompt",
    )(*([proj4] * 11), *weights)


def _mix_sample_kernel(ab_ref, ac_ref, ah_ref, su_ref, sv_ref, ca_ref, cg_ref,
                       ha_ref, hc_ref,
                       caw_ref, slg_ref, slb_ref, swd_ref, sbc_ref,
                       cw_ref, cb_ref, clg_ref, clb_ref,
                       y_ref, sta_ref, stc_ref, vn_ref,
                       ch_ref, z_ref, zc_ref, chx_ref, zx_ref, *, bs, t_new):
    ka, kc = SC_WIDTH - 1, CONF_WIDTH - 1

    ch_ref[...] = ac_ref[...] * ah_ref[...]
    z_ref[...] = ca_ref[...] * _sigmoid(cg_ref[...])
    vn_ref[...] = _layer_norm(sv_ref[...], slg_ref[...], slb_ref[...])

    for h in range(HEADS):
        lanes = slice(h * HEAD_DIM, (h + 1) * HEAD_DIM)
        mixed = jnp.dot(swd_ref[h].astype(BF16), vn_ref[:, lanes].astype(BF16),
                        preferred_element_type=F32) + sbc_ref[:, h:h + 1]
        y_ref[:, G + h * HEAD_DIM:G + (h + 1) * HEAD_DIM] = su_ref[:, lanes] * mixed

    caw = caw_ref[...]

    def per_sample(b, carry):
        r = pl.multiple_of(b * t_new, t_new)
        new = pl.ds(r, t_new)
        chx_ref[8 - ka:8, :] = ha_ref[b]
        chx_ref[8:8 + t_new, :] = ch_ref[new, :]
        conv = (caw[0:1] * chx_ref[6:6 + t_new, :] + caw[1:2] * chx_ref[7:7 + t_new, :]
                + caw[2:3] * chx_ref[8:8 + t_new, :])
        y_ref[new, 0:G] = ab_ref[new, :] * conv
        sta_ref[b] = chx_ref[8 + t_new - ka:8 + t_new, :]

        zx_ref[HALO - kc:HALO, :] = hc_ref[b]
        zx_ref[HALO:HALO + t_new, :] = z_ref[new, :]
        acc = None
        for k in range(CONF_WIDTH):
            term = cw_ref[k:k + 1, :] * zx_ref[HALO - kc + k:HALO - kc + k + t_new, :]
            acc = term if acc is None else acc + term
        zc_ref[new, :] = acc
        stc_ref[b] = zx_ref[HALO + t_new - kc:HALO + t_new, :]
        return carry

    lax.fori_loop(0, bs, per_sample, 0)
    zn = _layer_norm(zc_ref[...] + cb_ref[...], clg_ref[...], clb_ref[...])
    y_ref[:, 2 * G:3 * G] = zn * _sigmoid(zn)


def _mix_sample(proj, hist_a, hist_c, p, batch, t_new, bs=16):
    ka, kc = SC_WIDTH - 1, CONF_WIDTH - 1
    n = batch * t_new
    rows = bs * t_new
    tril = jnp.tril(jnp.ones((t_new, t_new), F32))
    w8 = p['sgu_w'][:, :t_new, :t_new] * tril
    swd = jax.vmap(lambda w: jnp.kron(jnp.eye(bs, dtype=F32), w))(w8)
    sbc = jnp.tile(p['sgu_b'][:, :t_new], (1, bs)).T

    def col(c):
        return pl.BlockSpec((None, rows, G), lambda i, c=c: (c, i, 0))

    def full(a):
        return pl.BlockSpec(a.shape, lambda i, nd=a.ndim: (0,) * nd)

    weights = [p['conv_a_w'], p['sgu_ln_g'], p['sgu_ln_b'], swd, sbc,
               p['conf_w'], p['conf_b'], p['conf_ln_g'], p['conf_ln_b']]
    return pl.pallas_call(
        functools.partial(_mix_sample_kernel, bs=bs, t_new=t_new),
        grid=(batch // bs,),
        in_specs=[col(c) for c in (COL_AB, COL_AC, COL_AH, COL_SU, COL_SV, COL_CA, COL_CG)]
                 + [pl.BlockSpec((bs, ka, G), lambda i: (i, 0, 0)),
                    pl.BlockSpec((bs, kc, G), lambda i: (i, 0, 0))]
                 + [full(a) for a in weights],
        out_specs=[pl.BlockSpec((rows, 3 * G), lambda i: (i, 0)),
                   pl.BlockSpec((bs, ka, G), lambda i: (i, 0, 0)),
                   pl.BlockSpec((bs, kc, G), lambda i: (i, 0, 0)),
                   pl.BlockSpec((rows, G), lambda i: (i, 0))],
        out_shape=[jax.ShapeDtypeStruct((n, 3 * G), F32),
                   jax.ShapeDtypeStruct((batch, ka, G), F32),
                   jax.ShapeDtypeStruct((batch, kc, G), F32),
                   jax.ShapeDtypeStruct((n, G), F32)],
        scratch_shapes=[pltpu.VMEM((rows, G), F32)] * 3
                       + [pltpu.VMEM((8 + t_new, G), F32), pltpu.VMEM((HALO + t_new, G), F32)],
        compiler_params=_params("parallel"),
        name="mix_sample",
    )(*([proj] * 7), hist_a, hist_c, *weights)


def _attn_prompt_kernel(q_ref, k_ref, v_ref, lam_ref, g_ref, o_ref,
                        kb_ref, vb_ref, q2_ref, m_ref, l_ref, acc_ref, *, lam_init, tq):
    qi = pl.program_id(2)

    @pl.when(qi == 0)
    def _():
        for jb in range(kb_ref.shape[0]):
            kb_ref[jb] = k_ref[:, jb * tq:(jb + 1) * tq].astype(BF16)
        vb_ref[...] = v_ref[...].astype(BF16)

    th = tq // 2
    lane = lax.broadcasted_iota(jnp.int32, (th, HEAD_DIM), 1)
    for u in range(2):
        q = q_ref[u * th:(u + 1) * th, :] * (ATT_SCALE * LOG2_E)
        q2_ref[u, 0:th, :] = jnp.where(lane < QK_DIM, q, 0.0).astype(BF16)
        q2_ref[u, th:2 * th, :] = jnp.where(lane >= QK_DIM, q, 0.0).astype(BF16)
    def step(u, jb, lo, n_keys, triangular=False, first=False):
        keys = pl.ds(pl.multiple_of(jb * tq, tq) + lo, n_keys)
        s = jnp.dot(q2_ref[u], kb_ref[jb, :, lo:lo + n_keys], preferred_element_type=F32)
        if triangular:
            row = lax.broadcasted_iota(jnp.int32, (2 * th, n_keys), 0) % th
            col = lax.broadcasted_iota(jnp.int32, (2 * th, n_keys), 1)
            s = jnp.where(col <= row, s, NEG_INF)
        m_cur = jnp.broadcast_to(jnp.max(s, axis=-1, keepdims=True), (2 * th, HEAD_DIM))
        m_new = m_cur if first else jnp.maximum(m_ref[u], m_cur)
        p = jnp.exp2(s - jnp.concatenate([m_new] * (n_keys // HEAD_DIM), axis=1))
        l_cur = jnp.broadcast_to(jnp.sum(p, axis=-1, keepdims=True), (2 * th, HEAD_DIM))
        pv = jnp.dot(p.astype(BF16), vb_ref[keys, :], preferred_element_type=F32)
        if first:
            l_ref[u] = l_cur
            acc_ref[u] = pv
        else:
            alpha = jnp.exp2(m_ref[u] - m_new)
            l_ref[u] = alpha * l_ref[u] + l_cur
            acc_ref[u] = alpha * acc_ref[u] + pv
        m_ref[u] = m_new

    step(0, qi, 0, th, triangular=True, first=True)
    step(1, qi, 0, th, first=True)
    step(1, qi, th, th, triangular=True)

    def full_block(j, carry):
        step(0, j, 0, tq)
        step(1, j, 0, tq)
        return carry

    lax.fori_loop(0, qi, full_block, 0)
    lam = _lambda(lam_ref, lam_init)
    for u in range(2):
        o = acc_ref[u] / l_ref[u]
        o_ref[u * th:(u + 1) * th, :] = _diff_norm(o[:th], o[th:], lam, g_ref[...],
                                                   lam_init).astype(o_ref.dtype)


def _attn_prompt(proj, k_t, v, p, batch, seq, lam_init, tq=512):
    proj4 = proj.reshape(N_MAIN, batch, seq, G)
    return pl.pallas_call(
        functools.partial(_attn_prompt_kernel, lam_init=lam_init, tq=tq),
        grid=(batch, HEADS, seq // tq),
        in_specs=[pl.BlockSpec((None, None, tq, HEAD_DIM), lambda b, h, i: (COL_Q, b, i, h)),
                  pl.BlockSpec((None, HEAD_DIM, seq), lambda b, h, i: (b, h, 0)),
                  pl.BlockSpec((None, seq, HEAD_DIM), lambda b, h, i: (b, 0, h)),
                  pl.BlockSpec((4, QK_DIM), lambda b, h, i: (0, 0)),
                  pl.BlockSpec((1, HEAD_DIM), lambda b, h, i: (0, 0))],
        out_specs=pl.BlockSpec((None, tq, HEAD_DIM), lambda b, h, i: (b, i, h)),
        out_shape=jax.ShapeDtypeStruct((batch, seq, G), BF16),
        scratch_shapes=[pltpu.VMEM((seq // tq, HEAD_DIM, tq), BF16),
                        pltpu.VMEM((seq, HEAD_DIM), BF16),
                        pltpu.VMEM((2, tq, HEAD_DIM), BF16)]
                       + [pltpu.VMEM((2, tq, HEAD_DIM), F32)] * 3,
        compiler_params=_params("parallel", "parallel", "arbitrary"),
        name="attn_prompt",
    )(proj4, k_t, v.reshape(batch, seq, G), p['lam'], p['subln_g'])


def _attn_sample_kernel(pt_ref, q_ref, kn_ref, vn_ref, lam_ref, g_ref, *rest,
                        lam_init, t_new, n_pages, page, ns):
    del pt_ref
    o_ref = rest[2 * ns * n_pages]
    lam = _lambda(lam_ref, lam_init)
    for s in range(ns):
        rows = slice(s * t_new, (s + 1) * t_new)
        _attn_one_sample(q_ref[rows, :], kn_ref[rows, :], vn_ref[rows, :], lam, g_ref[...],
                         rest[s * n_pages:(s + 1) * n_pages],
                         rest[(ns + s) * n_pages:(ns + s + 1) * n_pages],
                         o_ref.at[s], lam_init, t_new, n_pages, page)


def _attn_one_sample(q, k_new, v_new, lam, gain, k_refs, v_refs, o_ref,
                     lam_init, t_new, n_pages, page):
    n_rows = HEADS * 2 * t_new

    q = q * ATT_SCALE
    qt = jnp.concatenate([q] * (HEADS * 2), axis=0)
    r_i = lax.broadcasted_iota(jnp.int32, (n_rows, G), 0)
    c_i = lax.broadcasted_iota(jnp.int32, (n_rows, G), 1)
    qbig = jnp.where(c_i // QK_DIM == r_i // t_new, qt, 0.0).astype(BF16)

    s_past = [jnp.dot(qbig, k_refs[j][...].astype(BF16), preferred_element_type=F32)
              for j in range(n_pages)]
    pad = jnp.zeros((page - t_new, G), F32)
    s_new = lax.dot_general(qbig, jnp.concatenate([k_new, pad], axis=0).astype(BF16),
                            NT_DIMS, preferred_element_type=F32)
    r_n = lax.broadcasted_iota(jnp.int32, (n_rows, page), 0) % t_new
    c_n = lax.broadcasted_iota(jnp.int32, (n_rows, page), 1)
    s_new = jnp.where(c_n <= r_n, s_new, NEG_INF)

    m = s_new
    for s in s_past:
        m = jnp.maximum(m, s)
    m = jnp.max(m, axis=-1, keepdims=True)
    p_new = jnp.exp(s_new - m)
    l = jnp.sum(p_new, axis=-1, keepdims=True)
    acc = jnp.dot(p_new.astype(BF16), jnp.concatenate([v_new, pad], axis=0).astype(BF16),
                  preferred_element_type=F32)
    for j in range(n_pages):
        pj = jnp.exp(s_past[j] - m)
        l = l + jnp.sum(pj, axis=-1, keepdims=True)
        v_page = jnp.concatenate([v_refs[j][pl.ds(h, page, stride=HEADS), :]
                                  for h in range(HEADS)], axis=1)
        acc = acc + jnp.dot(pj.astype(BF16), v_page.astype(BF16), preferred_element_type=F32)
    o = acc / l
    for h in range(HEADS):
        lanes = slice(h * HEAD_DIM, (h + 1) * HEAD_DIM)
        r0 = h * 2 * t_new
        o_ref[:, lanes] = _diff_norm(o[r0:r0 + t_new, lanes], o[r0 + t_new:r0 + 2 * t_new, lanes],
                                     lam, gain, lam_init)


def _attn_sample(proj, cache_k, cache_v, page_table, layer, p, batch, t_new, lam_init, ns=2):
    n_pages = page_table.shape[1]
    page = cache_k.shape[2]
    ck = cache_k.transpose(0, 1, 3, 4, 5, 2).reshape(cache_k.shape[0], cache_k.shape[1], G, page)
    cv = cache_v.reshape(cache_v.shape[0], cache_v.shape[1], page * HEADS, HEAD_DIM)
    pt = page_table.reshape(-1)

    def new(c):
        return pl.BlockSpec((None, ns * t_new, G), lambda b, pt, c=c: (c, b, 0))

    def paged(j, shape):
        return pl.BlockSpec((None, None) + shape,
                            lambda b, pt, j=j: (layer, pt[b * ns * n_pages + j], 0, 0))

    n_in = ns * n_pages
    grid_spec = pltpu.PrefetchScalarGridSpec(
        num_scalar_prefetch=1,
        grid=(batch // ns,),
        in_specs=[new(COL_Q), new(COL_K), new(COL_V),
                  pl.BlockSpec((4, QK_DIM), lambda b, pt: (0, 0)),
                  pl.BlockSpec((1, HEAD_DIM), lambda b, pt: (0, 0))]
                 + [paged(j, (G, page)) for j in range(n_in)]
                 + [paged(j, (page * HEADS, HEAD_DIM)) for j in range(n_in)],
        out_specs=pl.BlockSpec((ns, t_new, G), lambda b, pt: (b, 0, 0)),
    )
    return pl.pallas_call(
        functools.partial(_attn_sample_kernel, lam_init=lam_init, t_new=t_new,
                          n_pages=n_pages, page=page, ns=ns),
        grid_spec=grid_spec,
        out_shape=jax.ShapeDtypeStruct((batch, t_new, G), F32),
        compiler_params=_params("parallel"),
        name="attn_sample",
    )(pt, proj, proj, proj, p['lam'], p['subln_g'], *([ck] * n_in), *([cv] * n_in))


def _outproj_kernel(x_ref, ya_ref, yd_ref, w_ref, g_ref, b_ref, o_ref):
    mix = jnp.dot(ya_ref[...].astype(BF16), w_ref[0:3 * G, :], preferred_element_type=F32)
    mix = mix + jnp.dot(yd_ref[...].astype(BF16), w_ref[3 * G:4 * G, :], preferred_element_type=F32)
    o_ref[...] = _layer_norm(DN_ALPHA * x_ref[...] + mix, g_ref[...], b_ref[...])


def _outproj(x, y_abc, y_d, p, layer, tm):
    n = x.shape[0]
    row = lambda i: (i, 0)
    fixed = lambda i: (0, 0)
    return pl.pallas_call(
        _outproj_kernel,
        grid=(n // tm,),
        in_specs=[pl.BlockSpec((tm, D_MODEL), row), pl.BlockSpec((tm, 3 * G), row),
                  pl.BlockSpec((tm, G), row),
                  pl.BlockSpec((None, D_MODEL, D_MODEL), lambda i: (layer, 0, 0),
                               pipeline_mode=pl.Buffered(1)),
                  pl.BlockSpec((1, D_MODEL), fixed), pl.BlockSpec((1, D_MODEL), fixed)],
        out_specs=pl.BlockSpec((tm, D_MODEL), row),
        out_shape=jax.ShapeDtypeStruct((n, D_MODEL), F32),
        compiler_params=_params("parallel"),
        name="outproj_ln",
    )(x, y_abc, y_d, p['w_out'], p['ln1_g'], p['ln1_b'])


def _swiglu_kernel(x_ref, wg_ref, wu_ref, wd_ref, g_ref, b_ref, o_ref, xb_ref):
    j = pl.program_id(1)

    @pl.when(j == 0)
    def _():
        xb_ref[...] = x_ref[...].astype(BF16)
        o_ref[...] = jnp.zeros_like(o_ref)

    xb = xb_ref[...]
    half = wg_ref.shape[1] // 2
    for c in range(2):
        cols = slice(c * half, (c + 1) * half)
        gate = jnp.dot(xb, wg_ref[:, cols], preferred_element_type=F32)
        up = jnp.dot(xb, wu_ref[:, cols], preferred_element_type=F32)
        h = (gate * _sigmoid(gate) * up).astype(BF16)
        o_ref[...] += jnp.dot(h, wd_ref[cols, :], preferred_element_type=F32)

    @pl.when(j == pl.num_programs(1) - 1)
    def _():
        o_ref[...] = _layer_norm(DN_ALPHA * x_ref[...] + o_ref[...], g_ref[...], b_ref[...])


def _swiglu(x, p, layer, tm, tf=512):
    n = x.shape[0]
    return pl.pallas_call(
        _swiglu_kernel,
        grid=(n // tm, D_FF // tf),
        in_specs=[pl.BlockSpec((tm, D_MODEL), lambda i, j: (i, 0)),
                  pl.BlockSpec((None, D_MODEL, tf), lambda i, j: (layer, 0, j)),
                  pl.BlockSpec((None, D_MODEL, tf), lambda i, j: (layer, 0, j)),
                  pl.BlockSpec((None, tf, D_MODEL), lambda i, j: (layer, j, 0)),
                  pl.BlockSpec((1, D_MODEL), lambda i, j: (0, 0)),
                  pl.BlockSpec((1, D_MODEL), lambda i, j: (0, 0))],
        out_specs=pl.BlockSpec((tm, D_MODEL), lambda i, j: (i, 0)),
        out_shape=jax.ShapeDtypeStruct((n, D_MODEL), F32),
        scratch_shapes=[pltpu.VMEM((tm, D_MODEL), BF16)],
        compiler_params=_params("parallel", "arbitrary"),
        name="swiglu_ln",
    )(x, p['w_gate'], p['w_up'], p['w_down'], p['ln2_g'], p['ln2_b'])


def kernel(x_prompt, x_sample, cache_k, cache_v, state_conv_a, state_conv_c, page_table, w_in, w_out, conv_a_w, sgu_ln_g, sgu_ln_b, sgu_w, sgu_b, conf_w, conf_b, conf_ln_g, conf_ln_b, lam_q1, lam_k1, lam_q2, lam_k2, subln_g, ln1_g, ln1_b, w_gate, w_up, w_down, ln2_g, ln2_b):
    bp, seq, _ = x_prompt.shape
    bd, t_new, _ = x_sample.shape
    xp = x_prompt.reshape(bp * seq, D_MODEL)
    xs = x_sample.reshape(bd * t_new, D_MODEL)
    outs = [[] for _ in range(9)]
    dense = {'w_in': w_in.astype(BF16), 'w_out': w_out.astype(BF16), 'w_gate': w_gate.astype(BF16),
             'w_up': w_up.astype(BF16), 'w_down': w_down.astype(BF16)}
    dense['w_k_t'] = jnp.swapaxes(dense['w_in'][:, :, COL_K * G:(COL_K + 1) * G], 1, 2)
    for l in range(DEPTH):
        row = lambda a: a[l].reshape(1, -1)
        p = {**dense,
             'conv_a_w': conv_a_w[l], 'sgu_ln_g': row(sgu_ln_g), 'sgu_ln_b': row(sgu_ln_b),
             'sgu_w': sgu_w[l], 'sgu_b': sgu_b[l], 'sgu_b_t': sgu_b[l].T,
             'conf_w': conf_w[l], 'conf_b': row(conf_b),
             'conf_ln_g': row(conf_ln_g), 'conf_ln_b': row(conf_ln_b),
             'lam': jnp.stack([lam_q1[l], lam_k1[l], lam_q2[l], lam_k2[l]]),
             'subln_g': row(subln_g), 'ln1_g': row(ln1_g), 'ln1_b': row(ln1_b),
             'ln2_g': row(ln2_g), 'ln2_b': row(ln2_b)}
        lam_init = 0.8 - 0.6 * math.exp(-0.3 * l)

        proj, k_t, v, v_rows = _inproj_prompt(xp, p['w_in'], p['w_k_t'], l, bp, seq, tm=1024)
        y_abc, a_p, c_p = _mix_prompt(proj, p, bp, seq)
        y_d = _attn_prompt(proj, k_t, v, p, bp, seq, lam_init)
        xp = _outproj(xp, y_abc.reshape(bp * seq, 3 * G), y_d.reshape(bp * seq, G), p, l, tm=512)
        xp = _swiglu(xp, p, l, tm=512)
        k_p = k_t.reshape(bp, HEADS, 2, QK_DIM, seq).transpose(0, 4, 1, 2, 3)
        v_p = v_rows.reshape(bp, seq, HEADS, HEAD_DIM)

        proj = _inproj(xs, p['w_in'], l, tm=512)
        y_abc, a_s, c_s, u_s = _mix_sample(proj, state_conv_a[l], state_conv_c[l], p, bd, t_new)
        u_s = u_s.reshape(bd, t_new, G)
        y_d = _attn_sample(proj, cache_k, cache_v, page_table, l, p, bd, t_new, lam_init)
        xs = _outproj(xs, y_abc, y_d.reshape(bd * t_new, G), p, l, tm=512)
        xs = _swiglu(xs, p, l, tm=512)
        k_s = proj[COL_K].reshape(bd, t_new, HEADS, 2, QK_DIM)
        v_s = proj[COL_V].reshape(bd, t_new, HEADS, HEAD_DIM)

        for acc, val in zip(outs, (k_p, v_p, k_s, v_s, a_p, a_s, c_p, c_s, u_s)):
            acc.append(val)
    return (xp.reshape(bp, seq, D_MODEL), xs.reshape(bd, t_new, D_MODEL),
            *(jnp.stack(o) for o in outs))
```
